```python
import jax, jax.numpy as jnp
from jax import lax
import numpy as np

D_MODEL = 1024
BATCH = 2
SEQ = 16384
DEPTH = 4

N_MIXERS = 2
N_POOL_LAYERS = (DEPTH + 1) // 2
N_MLA_LAYERS = DEPTH // 2
POOL_WINDOWS = (2, 4, 8, 16)
N_POOL_GROUPS = len(POOL_WINDOWS)
POOL_GROUP = D_MODEL // N_POOL_GROUPS
N_HEADS = D_MODEL // 128
QK_NOPE = 128
QK_ROPE = 64
QK_HEAD = QK_NOPE + QK_ROPE
V_HEAD = 128
Q_LORA = 3 * D_MODEL // 4
KV_LORA = D_MODEL // 4
ROPE_THETA = 10000.0
Q_BLOCK = 128
D_FF = 2816
FFN_HALF = 0.5
EPS = 1e-6

kernel_name = "hybrid_pool_mla_macaron_trunk"


def rmsnorm(x, gain):
    x32 = x.astype(jnp.float32)
    y = x32 * lax.rsqrt(jnp.mean(x32 * x32, axis=-1, keepdims=True) + EPS)
    return (y * gain.astype(jnp.float32)).astype(x.dtype)


def swiglu(h, w_gate, w_up, w_down):
    return (jax.nn.silu(h @ w_gate) * (h @ w_up)) @ w_down


def pool_mixer(h, w, scale):
    B, S, D = h.shape
    u = h.astype(jnp.float32).reshape(B, S, N_POOL_GROUPS, POOL_GROUP)
    cs = jnp.pad(jnp.cumsum(u, axis=1), ((0, 0), (1, 0), (0, 0), (0, 0)))
    sums = []
    for g, w_len in enumerate(POOL_WINDOWS):
        c = cs[:, :, g]
        lower = jnp.pad(c[:, :S + 1 - w_len], ((0, 0), (w_len - 1, 0), (0, 0)))
        sums.append(c[:, 1:] - lower)
    window_sum = jnp.stack(sums, axis=2)
    t = jnp.arange(S)
    count = jnp.minimum(t[:, None] + 1, jnp.array(POOL_WINDOWS, jnp.int32)[None, :])
    pooled = window_sum / count.astype(jnp.float32)[None, :, :, None] - u
    y = jnp.einsum('bsgc,gcd->bsgd', pooled.astype(h.dtype), w).reshape(B, S, D)
    return y * scale


def rope_tail(x, cos, sin):
    x_nope, x_pe = jnp.split(x, [QK_NOPE], axis=-1)
    x1, x2 = jnp.split(x_pe, 2, axis=-1)
    c = cos[:, :, None, :].astype(x.dtype)
    s = sin[:, :, None, :].astype(x.dtype)
    return jnp.concatenate([x_nope, x1 * c - x2 * s, x2 * c + x1 * s], axis=-1)


def causal_attention(q, k, v):
    B, S, H, Dq = q.shape
    nb = S // Q_BLOCK
    qb = q.reshape(B, nb, Q_BLOCK, H, Dq).transpose(1, 0, 3, 2, 4)
    kt = k.transpose(0, 2, 1, 3)
    vt = v.transpose(0, 2, 1, 3)
    kpos = jnp.arange(S)
    scale = QK_HEAD ** -0.5

    def one_block(args):
        q_blk, blk = args
        s = jnp.einsum('bhqd,bhkd->bhqk', q_blk, kt,
                       preferred_element_type=jnp.float32) * scale
        qpos = blk * Q_BLOCK + jnp.arange(Q_BLOCK)
        s = jnp.where(kpos[None, :] <= qpos[:, None], s, -jnp.inf)
        p = jax.nn.softmax(s, axis=-1).astype(vt.dtype)
        return jnp.einsum('bhqk,bhkv->bhqv', p, vt)

    out = lax.map(one_block, (qb, jnp.arange(nb)))
    return out.transpose(1, 0, 3, 2, 4).reshape(B, S, H, V_HEAD)


def mla_mixer(h, cos, sin, w_in, q_norm, w_q_up, kv_norm, w_kv_up,
              q_head_norm, k_head_norm, w_out):
    B, S, _ = h.shape
    lat = h @ w_in
    cq, ckv, k_pe = jnp.split(lat, [Q_LORA, Q_LORA + KV_LORA], axis=-1)
    q = (rmsnorm(cq, q_norm) @ w_q_up).reshape(B, S, N_HEADS, QK_HEAD)
    kv = (rmsnorm(ckv, kv_norm) @ w_kv_up).reshape(B, S, N_HEADS, QK_NOPE + V_HEAD)
    k_nope, v = jnp.split(kv, [QK_NOPE], axis=-1)
    k = jnp.concatenate(
        [k_nope, jnp.broadcast_to(k_pe[:, :, None, :], (B, S, N_HEADS, QK_ROPE))], axis=-1)
    q = rope_tail(rmsnorm(q, q_head_norm), cos, sin)
    k = rope_tail(rmsnorm(k, k_head_norm), cos, sin)
    o = causal_attention(q, k, v)
    return o.reshape(B, S, N_HEADS * V_HEAD) @ w_out


def setup_inputs(seed: int = 0) -> dict:
    key = jax.random.key(seed)
    ks = jax.random.split(key, 24)
    f32 = jnp.float32

    def dense(k, shape, fan_in):
        return jax.random.normal(k, shape, f32) * fan_in ** -0.5

    def gain(k, shape):
        return 1.0 + 0.05 * jax.random.normal(k, shape, f32)

    x = jax.random.normal(ks[0], (BATCH, SEQ, D_MODEL), f32)
    offsets = jax.random.randint(ks[1], (BATCH, 1), 0, 4096, dtype=jnp.int32)
    positions = (jnp.arange(SEQ, dtype=jnp.int32)[None, :] + offsets).astype(jnp.int32)
    return {
        "x": x,
        "positions": positions,
        "ffn1_norm": gain(ks[2], (DEPTH, D_MODEL)),
        "ffn1_w_gate": dense(ks[3], (DEPTH, D_MODEL, D_FF), D_MODEL),
        "ffn1_w_up": dense(ks[4], (DEPTH, D_MODEL, D_FF), D_MODEL),
        "ffn1_w_down": dense(ks[5], (DEPTH, D_FF, D_MODEL), D_FF),
        "mix_norm": gain(ks[6], (DEPTH, D_MODEL)),
        "pool_w": dense(ks[7], (N_POOL_LAYERS, N_POOL_GROUPS, POOL_GROUP, POOL_GROUP), POOL_GROUP),
        "pool_scale": gain(ks[8], (N_POOL_LAYERS, D_MODEL)),
        "mla_w_in": dense(ks[9], (N_MLA_LAYERS, D_MODEL, Q_LORA + KV_LORA + QK_ROPE), D_MODEL),
        "mla_q_norm": gain(ks[10], (N_MLA_LAYERS, Q_LORA)),
        "mla_w_q_up": dense(ks[11], (N_MLA_LAYERS, Q_LORA, N_HEADS * QK_HEAD), Q_LORA),
        "mla_kv_norm": gain(ks[12], (N_MLA_LAYERS, KV_LORA)),
        "mla_w_kv_up": dense(ks[13], (N_MLA_LAYERS, KV_LORA, N_HEADS * (QK_NOPE + V_HEAD)), KV_LORA),
        "mla_q_head_norm": gain(ks[14], (N_MLA_LAYERS, QK_HEAD)),
        "mla_k_head_norm": gain(ks[15], (N_MLA_LAYERS, QK_HEAD)),
        "mla_w_out": dense(ks[16], (N_MLA_LAYERS, N_HEADS * V_HEAD, D_MODEL), N_HEADS * V_HEAD),
        "ffn2_norm": gain(ks[17], (DEPTH, D_MODEL)),
        "ffn2_w_gate": dense(ks[18], (DEPTH, D_MODEL, D_FF), D_MODEL),
        "ffn2_w_up": dense(ks[19], (DEPTH, D_MODEL, D_FF), D_MODEL),
        "ffn2_w_down": dense(ks[20], (DEPTH, D_FF, D_MODEL), D_FF),
    }


def reference(x, positions, ffn1_norm, ffn1_w_gate, ffn1_w_up, ffn1_w_down, mix_norm,
              pool_w, pool_scale, mla_w_in, mla_q_norm, mla_w_q_up, mla_kv_norm,
              mla_w_kv_up, mla_q_head_norm, mla_k_head_norm, mla_w_out,
              ffn2_norm, ffn2_w_gate, ffn2_w_up, ffn2_w_down):
    inv_freq = 1.0 / (ROPE_THETA ** (jnp.arange(0, QK_ROPE, 2, dtype=jnp.float32) / QK_ROPE))
    ang = positions.astype(jnp.float32)[..., None] * inv_freq
    cos, sin = jnp.cos(ang), jnp.sin(ang)

    for i in range(DEPTH):
        h = rmsnorm(x, ffn1_norm[i])
        x = x + FFN_HALF * swiglu(h, ffn1_w_gate[i], ffn1_w_up[i], ffn1_w_down[i])
        h = rmsnorm(x, mix_norm[i])
        j = i // N_MIXERS
        if i % N_MIXERS == 0:
            x = x + pool_mixer(h, pool_w[j], pool_scale[j])
        else:
            x = x + mla_mixer(h, cos, sin, mla_w_in[j], mla_q_norm[j], mla_w_q_up[j],
                              mla_kv_norm[j], mla_w_kv_up[j], mla_q_head_norm[j],
                              mla_k_head_norm[j], mla_w_out[j])
        h = rmsnorm(x, ffn2_norm[i])
        x = x + FFN_HALF * swiglu(h, ffn2_w_gate[i], ffn2_w_up[i], ffn2_w_down[i])
    return x
```

```python
import functools
import math

import jax
import jax.numpy as jnp
from jax import lax
from jax.experimental import pallas as pl
from jax.experimental.pallas import tpu as pltpu

F32 = jnp.float32
BF16 = jnp.bfloat16

EPS = 1e-6
FFN_HALF = 0.5
POOL_WINDOWS = (2, 4, 8, 16)
POOL_HALO = 16
QK_NOPE = 128
QK_ROPE = 64
QK_HEAD = QK_NOPE + QK_ROPE
QK_PAD = 256
V_HEAD = 128
ROPE_THETA = 10000.0
LOG2E = math.log2(math.e)

VMEM_LIMIT = 56 * 1024 * 1024


def _cparams(sem):
    return pltpu.CompilerParams(dimension_semantics=sem, vmem_limit_bytes=VMEM_LIMIT)


def _resident(shape):
    nd = len(shape)
    return pl.BlockSpec(shape, lambda *_: (0,) * nd, pipeline_mode=pl.Buffered(1))


def _rms_rows(x, gain):
    ms = jnp.mean(x * x, axis=-1, keepdims=True)
    return x * lax.rsqrt(ms + EPS) * gain


def _ffn_kernel(x_ref, g_ref, wg_ref, wu_ref, wd_ref, o_ref):
    x = x_ref[...]
    hn = _rms_rows(x, g_ref[...]).astype(BF16)
    g = jnp.dot(hn, wg_ref[...], preferred_element_type=F32)
    u = jnp.dot(hn, wu_ref[...], preferred_element_type=F32)
    a = (g / (1.0 + jnp.exp(-g)) * u).astype(BF16)
    y = jnp.dot(a, wd_ref[...], preferred_element_type=F32)
    o_ref[...] = x + FFN_HALF * y


def _ffn(x2, gain, wg, wu, wd, tm):
    T, D = x2.shape
    F = wg.shape[1]
    return pl.pallas_call(
        _ffn_kernel,
        grid=(T // tm,),
        in_specs=[
            pl.BlockSpec((tm, D), lambda i: (i, 0)),
            _resident((1, D)),
            _resident((D, F)),
            _resident((D, F)),
            _resident((F, D)),
        ],
        out_specs=pl.BlockSpec((tm, D), lambda i: (i, 0)),
        out_shape=jax.ShapeDtypeStruct((T, D), F32),
        compiler_params=_cparams(("parallel",)),
        name="ffn",
    )(x2, gain.reshape(1, D), wg, wu, wd)


def _pool_kernel(x_ref, halo_ref, g_ref, w_ref, sc_ref, o_ref, hbuf, *, tm):
    i = pl.program_id(1)
    gain = g_ref[...]
    x = x_ref[0]
    hn = _rms_rows(x, gain)
    hh = _rms_rows(halo_ref[0], gain)
    hbuf[0:POOL_HALO] = jnp.where(i > 0, hh, 0.0)
    hbuf[POOL_HALO:POOL_HALO + tm] = hn
    t = i * tm + lax.broadcasted_iota(jnp.int32, (tm, 1), 0)
    C = w_ref.shape[1]
    for g, w_len in enumerate(POOL_WINDOWS):
        cols = slice(g * C, (g + 1) * C)
        u = hn[:, cols]
        wsum = u
        for j in range(1, w_len):
            wsum = wsum + hbuf[POOL_HALO - j:POOL_HALO - j + tm, cols]
        count = jnp.minimum(t + 1, w_len).astype(F32)
        pooled = (wsum / count - u).astype(BF16)
        y = jnp.dot(pooled, w_ref[g], preferred_element_type=F32)
        o_ref[0, :, cols] = x[:, cols] + y * sc_ref[:, cols]


def _pool(x3, gain, w, scale, tm):
    B, S, D = x3.shape
    G, C, _ = w.shape
    hb = tm // POOL_HALO
    return pl.pallas_call(
        functools.partial(_pool_kernel, tm=tm),
        grid=(B, S // tm),
        in_specs=[
            pl.BlockSpec((1, tm, D), lambda b, i: (b, i, 0)),
            pl.BlockSpec((1, POOL_HALO, D), lambda b, i: (b, jnp.maximum(i * hb - 1, 0), 0)),
            _resident((1, D)),
            _resident((G, C, C)),
            _resident((1, D)),
        ],
        out_specs=pl.BlockSpec((1, tm, D), lambda b, i: (b, i, 0)),
        out_shape=jax.ShapeDtypeStruct((B, S, D), F32),
        scratch_shapes=[pltpu.VMEM((POOL_HALO + tm, D), F32)],
        compiler_params=_cparams(("parallel", "parallel")),
        name="pool",
    )(x3, x3, gain.reshape(1, D), w, scale.reshape(1, D))


def _rms_cols(xT, gain_col, n):
    ms = jnp.sum(xT * xT, axis=0, keepdims=True) * (1.0 / n)
    return xT * lax.rsqrt(ms + EPS) * gain_col


def _mla_pre_kernel(x_ref, pos_ref, invf_ref, gmix_ref, winT_ref, gq_ref, wqT_ref, gkv_ref, wkvT_ref,
                    gqh_ref, gkh_ref, qT_ref, k_ref, vT_ref, *, n_heads, q_lora, kv_lora, q_scale):
    tm = x_ref.shape[1]
    hn = _rms_rows(x_ref[0], gmix_ref[...]).astype(BF16)
    latT = lax.dot_general(winT_ref[...], hn, (((1,), (1,)), ((), ())), preferred_element_type=F32)
    cqT = latT[:q_lora]
    ckvT = latT[q_lora:q_lora + kv_lora]
    kpeT = latT[q_lora + kv_lora:]
    qT = jnp.dot(wqT_ref[...], _rms_cols(cqT, gq_ref[...], q_lora).astype(BF16),
                 preferred_element_type=F32)
    kvT = jnp.dot(wkvT_ref[...], _rms_cols(ckvT, gkv_ref[...], kv_lora).astype(BF16),
                  preferred_element_type=F32)

    ang = pos_ref[0].astype(F32) * invf_ref[...]
    cos, sin = jnp.cos(ang), jnp.sin(ang)
    half = QK_ROPE // 2
    gqh = gqh_ref[...]
    gkh = gkh_ref[...]
    kpe_ss = jnp.sum(kpeT * kpeT, axis=0, keepdims=True)
    pad = jnp.zeros((QK_PAD - QK_HEAD, tm), F32)

    def rope(x1, x2):
        return x1 * cos - x2 * sin, x2 * cos + x1 * sin

    for h in range(n_heads):
        qh = qT[h * QK_HEAD:(h + 1) * QK_HEAD]
        ms = jnp.sum(qh * qh, axis=0, keepdims=True) * (1.0 / QK_HEAD)
        qn = qh * (lax.rsqrt(ms + EPS) * q_scale) * gqh
        r1, r2 = rope(qn[QK_NOPE:QK_NOPE + half], qn[QK_NOPE + half:])
        qT_ref[0, h] = jnp.concatenate([qn[:QK_NOPE], r1, r2, pad], axis=0).astype(BF16)

        kn = kvT[h * (QK_NOPE + V_HEAD):h * (QK_NOPE + V_HEAD) + QK_NOPE]
        v = kvT[h * (QK_NOPE + V_HEAD) + QK_NOPE:(h + 1) * (QK_NOPE + V_HEAD)]
        ms = (jnp.sum(kn * kn, axis=0, keepdims=True) + kpe_ss) * (1.0 / QK_HEAD)
        rs = lax.rsqrt(ms + EPS)
        r1, r2 = rope(kpeT[:half] * rs * gkh[QK_NOPE:QK_NOPE + half],
                      kpeT[half:] * rs * gkh[QK_NOPE + half:])
        k_ref[0, h, 0, :, :QK_NOPE] = (kn * rs * gkh[:QK_NOPE]).T.astype(BF16)
        k_ref[0, h, 0, :, QK_NOPE:] = jnp.concatenate([r1, r2, pad], axis=0).T.astype(BF16)
        vT_ref[0, h, 0] = v.astype(BF16)


def _mla_pre(x3, pos3, invf, gmix, winT, gq, wqT, gkv, wkvT, gqh, gkh, n_heads, tm):
    B, S, D = x3.shape
    q_lora, kv_lora = wqT.shape[1], wkvT.shape[1]
    n_t = S // tm
    kern = functools.partial(_mla_pre_kernel, n_heads=n_heads, q_lora=q_lora, kv_lora=kv_lora,
                             q_scale=QK_HEAD ** -0.5 * LOG2E)
    return pl.pallas_call(
        kern,
        grid=(B, n_t),
        in_specs=[
            pl.BlockSpec((1, tm, D), lambda b, i: (b, i, 0)),
            pl.BlockSpec((1, 1, tm), lambda b, i: (b, 0, i)),
            _resident(invf.shape),
            _resident((1, D)),
            _resident(winT.shape),
            _resident((q_lora, 1)),
            _resident(wqT.shape),
            _resident((kv_lora, 1)),
            _resident(wkvT.shape),
            _resident((QK_HEAD, 1)),
            _resident((QK_HEAD, 1)),
        ],
        out_specs=[
            pl.BlockSpec((1, n_heads, QK_PAD, tm), lambda b, i: (b, 0, 0, i)),
            pl.BlockSpec((1, n_heads, 1, tm, QK_PAD), lambda b, i: (b, 0, i, 0, 0)),
            pl.BlockSpec((1, n_heads, 1, V_HEAD, tm), lambda b, i: (b, 0, i, 0, 0)),
        ],
        out_shape=[
            jax.ShapeDtypeStruct((B, n_heads, QK_PAD, S), BF16),
            jax.ShapeDtypeStruct((B, n_heads, n_t, tm, QK_PAD), BF16),
            jax.ShapeDtypeStruct((B, n_heads, n_t, V_HEAD, tm), BF16),
        ],
        compiler_params=_cparams(("parallel", "parallel")),
        name="mla_pre",
    )(x3, pos3, invf, gmix.reshape(1, D), winT, gq.reshape(-1, 1), wqT, gkv.reshape(-1, 1), wkvT,
      gqh.reshape(-1, 1), gkh.reshape(-1, 1))


def _attn_kernel(qT_ref, k_ref, vT_ref, o_ref, m_ref, l_ref, acc_ref, *, tq):
    i = pl.program_id(2)
    qT = qT_ref[0, 0]
    m_ref[...] = jnp.full(m_ref.shape, -jnp.inf, F32)
    l_ref[...] = jnp.zeros(l_ref.shape, F32)
    acc_ref[...] = jnp.zeros(acc_ref.shape, F32)

    def block(j, masked):
        sT = jnp.dot(k_ref[0, 0, j], qT, preferred_element_type=F32)
        if masked:
            key = lax.broadcasted_iota(jnp.int32, sT.shape, 0)
            qry = lax.broadcasted_iota(jnp.int32, sT.shape, 1)
            sT = jnp.where(key <= qry, sT, -jnp.inf)
        m_old = m_ref[...]
        m_new = jnp.maximum(m_old, jnp.max(sT, axis=0, keepdims=True))
        alpha = jnp.exp2(m_old - m_new)
        p = jnp.exp2(sT - m_new)
        l_ref[...] = alpha * l_ref[...] + jnp.sum(p, axis=0, keepdims=True)
        acc_ref[...] = alpha * acc_ref[...] + jnp.dot(vT_ref[0, 0, j], p.astype(BF16),
                                                      preferred_element_type=F32)
        m_ref[...] = m_new

    def body(j, carry):
        block(j, False)
        return carry

    lax.fori_loop(0, i, body, 0)
    block(i, True)
    o_ref[0] = (acc_ref[...] / l_ref[...]).T.astype(o_ref.dtype)


def _attn(qT, k, vT, tq):
    B, H, _, S = qT.shape
    n_t = S // tq
    return pl.pallas_call(
        functools.partial(_attn_kernel, tq=tq),
        grid=(B, H, n_t),
        in_specs=[
            pl.BlockSpec((1, 1, QK_PAD, tq), lambda b, h, i: (b, h, 0, i)),
            pl.BlockSpec((1, 1, n_t, tq, QK_PAD), lambda b, h, i: (b, h, 0, 0, 0)),
            pl.BlockSpec((1, 1, n_t, V_HEAD, tq), lambda b, h, i: (b, h, 0, 0, 0)),
        ],
        out_specs=pl.BlockSpec((1, tq, V_HEAD), lambda b, h, i: (b, i, h)),
        out_shape=jax.ShapeDtypeStruct((B, S, H * V_HEAD), BF16),
        scratch_shapes=[pltpu.VMEM((1, tq), F32), pltpu.VMEM((1, tq), F32), pltpu.VMEM((V_HEAD, tq), F32)],
        compiler_params=_cparams(("parallel", "parallel", "arbitrary")),
        name="attn",
    )(qT, k, vT)


def _out_proj_kernel(x_ref, o_ref, w_ref, y_ref):
    y_ref[...] = x_ref[...] + jnp.dot(o_ref[...], w_ref[...], preferred_element_type=F32)


def _out_proj(x2, o2, w, tm):
    T, D = x2.shape
    return pl.pallas_call(
        _out_proj_kernel,
        grid=(T // tm,),
        in_specs=[
            pl.BlockSpec((tm, D), lambda i: (i, 0)),
            pl.BlockSpec((tm, o2.shape[1]), lambda i: (i, 0)),
            _resident(w.shape),
        ],
        out_specs=pl.BlockSpec((tm, D), lambda i: (i, 0)),
        out_shape=jax.ShapeDtypeStruct((T, D), F32),
        compiler_params=_cparams(("parallel",)),
        name="out_proj",
    )(x2, o2, w)


def _tile(n, want):
    t = min(n, want)
    assert n % t == 0, (n, t)
    return t


def kernel(x, positions, ffn1_norm, ffn1_w_gate, ffn1_w_up, ffn1_w_down, mix_norm, pool_w, pool_scale,
           mla_w_in, mla_q_norm, mla_w_q_up, mla_kv_norm, mla_w_kv_up, mla_q_head_norm, mla_k_head_norm,
           mla_w_out, ffn2_norm, ffn2_w_gate, ffn2_w_up, ffn2_w_down):
    B, S, D = x.shape
    depth = ffn1_norm.shape[0]
    n_heads = mla_w_out.shape[1] // V_HEAD
    T = B * S
    tm_ffn = _tile(T, 512)
    tm_pool = _tile(S, 512)
    tq = _tile(S, 512)
    tm_out = _tile(T, 1024)

    invf = (1.0 / (ROPE_THETA ** (jnp.arange(0, QK_ROPE, 2, dtype=F32) / QK_ROPE))).reshape(-1, 1)
    pos3 = positions.reshape(B, 1, S)

    for i in range(depth):
        x = _ffn(x.reshape(T, D), ffn1_norm[i], ffn1_w_gate[i].astype(BF16), ffn1_w_up[i].astype(BF16),
                 ffn1_w_down[i].astype(BF16), tm_ffn).reshape(B, S, D)
        j = i // 2
        if i % 2 == 0:
            x = _pool(x, mix_norm[i], pool_w[j].astype(BF16), pool_scale[j], tm_pool)
        else:
            qT, k, vT = _mla_pre(x, pos3, invf, mix_norm[i], mla_w_in[j].T.astype(BF16), mla_q_norm[j],
                                 mla_w_q_up[j].T.astype(BF16), mla_kv_norm[j], mla_w_kv_up[j].T.astype(BF16),
                                 mla_q_head_norm[j], mla_k_head_norm[j], n_heads, tq)
            o = _attn(qT, k, vT, tq)
            x = _out_proj(x.reshape(T, D), o.reshape(T, n_heads * V_HEAD), mla_w_out[j].astype(BF16),
                          tm_out).reshape(B, S, D)
        x = _ffn(x.reshape(T, D), ffn2_norm[i], ffn2_w_gate[i].astype(BF16), ffn2_w_up[i].astype(BF16),
                 ffn2_w_down[i].astype(BF16), tm_ffn).reshape(B, S, D)
    return x
```

```python
import functools
import math

import jax
import jax.numpy as jnp
from jax import lax
from jax.experimental import pallas as pl
from jax.experimental.pallas import tpu as pltpu

F32 = jnp.float32
BF16 = jnp.bfloat16

EPS = 1e-6
FFN_HALF = 0.5
POOL_WINDOWS = (2, 4, 8, 16)
POOL_HALO = 16
QK_NOPE = 128
QK_ROPE = 64
QK_HEAD = QK_NOPE + QK_ROPE
QK_PAD = 256
V_HEAD = 128
ROPE_THETA = 10000.0
LOG2E = math.log2(math.e)
Q_SCALE = QK_HEAD ** -0.5 * LOG2E
SUBLANES = 8
F32_SAFE_EXP2 = 120.0

VMEM_LIMIT = 56 * 1024 * 1024


def _cparams(sem):
    return pltpu.CompilerParams(dimension_semantics=sem, vmem_limit_bytes=VMEM_LIMIT)


def _resident(shape):
    nd = len(shape)
    return pl.BlockSpec(shape, lambda *_: (0,) * nd, pipeline_mode=pl.Buffered(1))


def _rms_rows(x, gain):
    ms = jnp.mean(x * x, axis=-1, keepdims=True)
    return x * lax.rsqrt(ms + EPS) * gain


def _ffn_kernel(x_ref, g_ref, wg_ref, wu_ref, wd_ref, o_ref):
    x = x_ref[...]
    hn = _rms_rows(x, g_ref[...]).astype(BF16)
    g = jnp.dot(hn, wg_ref[...], preferred_element_type=F32)
    u = jnp.dot(hn, wu_ref[...], preferred_element_type=F32)
    a = (g / (1.0 + jnp.exp(-g)) * u).astype(BF16)
    y = jnp.dot(a, wd_ref[...], preferred_element_type=F32)
    o_ref[...] = x + FFN_HALF * y


def _ffn(x2, gain, wg, wu, wd, tm):
    T, D = x2.shape
    F = wg.shape[1]
    return pl.pallas_call(
        _ffn_kernel,
        grid=(T // tm,),
        in_specs=[
            pl.BlockSpec((tm, D), lambda i: (i, 0)),
            _resident((1, D)),
            _resident((D, F)),
            _resident((D, F)),
            _resident((F, D)),
        ],
        out_specs=pl.BlockSpec((tm, D), lambda i: (i, 0)),
        out_shape=jax.ShapeDtypeStruct((T, D), F32),
        compiler_params=_cparams(("parallel",)),
        name="ffn",
    )(x2, gain.reshape(1, D), wg, wu, wd)


def _pool_kernel(x_ref, halo_ref, g_ref, w_ref, sc_ref, o_ref, hbuf, *, tm):
    i = pl.program_id(1)
    gain = g_ref[...]
    x = x_ref[0]
    hn = _rms_rows(x, gain)
    hh = _rms_rows(halo_ref[0], gain)
    hbuf[0:POOL_HALO] = jnp.where(i > 0, hh, 0.0)
    hbuf[POOL_HALO:POOL_HALO + tm] = hn
    t = i * tm + lax.broadcasted_iota(jnp.int32, (tm, 1), 0)
    C = w_ref.shape[1]
    for g, w_len in enumerate(POOL_WINDOWS):
        cols = slice(g * C, (g + 1) * C)
        u = hn[:, cols]
        wsum = u
        for j in range(1, w_len):
            wsum = wsum + hbuf[POOL_HALO - j:POOL_HALO - j + tm, cols]
        count = jnp.minimum(t + 1, w_len).astype(F32)
        pooled = (wsum / count - u).astype(BF16)
        y = jnp.dot(pooled, w_ref[g], preferred_element_type=F32)
        o_ref[0, :, cols] = x[:, cols] + y * sc_ref[:, cols]


def _pool(x3, gain, w, scale, tm):
    B, S, D = x3.shape
    G, C, _ = w.shape
    hb = tm // POOL_HALO
    return pl.pallas_call(
        functools.partial(_pool_kernel, tm=tm),
        grid=(B, S // tm),
        in_specs=[
            pl.BlockSpec((1, tm, D), lambda b, i: (b, i, 0)),
            pl.BlockSpec((1, POOL_HALO, D), lambda b, i: (b, jnp.maximum(i * hb - 1, 0), 0)),
            _resident((1, D)),
            _resident((G, C, C)),
            _resident((1, D)),
        ],
        out_specs=pl.BlockSpec((1, tm, D), lambda b, i: (b, i, 0)),
        out_shape=jax.ShapeDtypeStruct((B, S, D), F32),
        scratch_shapes=[pltpu.VMEM((POOL_HALO + tm, D), F32)],
        compiler_params=_cparams(("parallel", "parallel")),
        name="pool",
    )(x3, x3, gain.reshape(1, D), w, scale.reshape(1, D))


def _rms_cols(xT, gain_col, n):
    ms = jnp.sum(xT * xT, axis=0, keepdims=True) * (1.0 / n)
    return xT * lax.rsqrt(ms + EPS) * gain_col


def _mla_pre_kernel(bound_ref, x_ref, pos_ref, invf_ref, gmix_ref, winT_ref, gq_ref, wqT_ref, gkv_ref, wkvT_ref,
                    gqh_ref, gkh_ref, qT_ref, k_ref, vT_ref, *, n_heads, q_lora, kv_lora, q_scale):
    tm = x_ref.shape[1]
    hn = _rms_rows(x_ref[0], gmix_ref[...]).astype(BF16)
    latT = lax.dot_general(winT_ref[...], hn, (((1,), (1,)), ((), ())), preferred_element_type=F32)
    cqT = latT[:q_lora]
    ckvT = latT[q_lora:q_lora + kv_lora]
    kpeT = latT[q_lora + kv_lora:]
    qT = jnp.dot(wqT_ref[...], _rms_cols(cqT, gq_ref[...], q_lora).astype(BF16),
                 preferred_element_type=F32)
    kvT = jnp.dot(wkvT_ref[...], _rms_cols(ckvT, gkv_ref[...], kv_lora).astype(BF16),
                  preferred_element_type=F32)

    ang = pos_ref[0].astype(F32) * invf_ref[...]
    cos, sin = jnp.cos(ang), jnp.sin(ang)
    half = QK_ROPE // 2
    gqh = gqh_ref[...]
    gkh = gkh_ref[...]
    kpe_ss = jnp.sum(kpeT * kpeT, axis=0, keepdims=True)
    first = lax.broadcasted_iota(jnp.int32, (QK_PAD - QK_HEAD, tm), 0) == 0
    q_pad = jnp.where(first, 1.0, 0.0)
    k_pad = jnp.where(first, -bound_ref[0], 0.0)

    def rope(x1, x2):
        return x1 * cos - x2 * sin, x2 * cos + x1 * sin

    for h in range(n_heads):
        qh = qT[h * QK_HEAD:(h + 1) * QK_HEAD]
        ms = jnp.sum(qh * qh, axis=0, keepdims=True) * (1.0 / QK_HEAD)
        qn = qh * (lax.rsqrt(ms + EPS) * q_scale) * gqh
        r1, r2 = rope(qn[QK_NOPE:QK_NOPE + half], qn[QK_NOPE + half:])
        qT_ref[0, h] = jnp.concatenate([qn[:QK_NOPE], r1, r2, q_pad], axis=0).astype(BF16)

        kn = kvT[h * (QK_NOPE + V_HEAD):h * (QK_NOPE + V_HEAD) + QK_NOPE]
        v = kvT[h * (QK_NOPE + V_HEAD) + QK_NOPE:(h + 1) * (QK_NOPE + V_HEAD)]
        ms = (jnp.sum(kn * kn, axis=0, keepdims=True) + kpe_ss) * (1.0 / QK_HEAD)
        rs = lax.rsqrt(ms + EPS)
        r1, r2 = rope(kpeT[:half] * rs * gkh[QK_NOPE:QK_NOPE + half],
                      kpeT[half:] * rs * gkh[QK_NOPE + half:])
        k_ref[0, h, 0, :, :QK_NOPE] = (kn * rs * gkh[:QK_NOPE]).T.astype(BF16)
        k_ref[0, h, 0, :, QK_NOPE:] = jnp.concatenate([r1, r2, k_pad], axis=0).T.astype(BF16)
        vT_ref[0, h, 0] = v.astype(BF16)


def _mla_pre(bound, x3, pos3, invf, gmix, winT, gq, wqT, gkv, wkvT, gqh, gkh, n_heads, tm):
    B, S, D = x3.shape
    q_lora, kv_lora = wqT.shape[1], wkvT.shape[1]
    n_t = S // tm
    kern = functools.partial(_mla_pre_kernel, n_heads=n_heads, q_lora=q_lora, kv_lora=kv_lora,
                             q_scale=Q_SCALE)
    return pl.pallas_call(
        kern,
        grid=(B, n_t),
        in_specs=[
            pl.BlockSpec(memory_space=pltpu.SMEM),
            pl.BlockSpec((1, tm, D), lambda b, i: (b, i, 0)),
            pl.BlockSpec((1, 1, tm), lambda b, i: (b, 0, i)),
            _resident(invf.shape),
            _resident((1, D)),
            _resident(winT.shape),
            _resident((q_lora, 1)),
            _resident(wqT.shape),
            _resident((kv_lora, 1)),
            _resident(wkvT.shape),
            _resident((QK_HEAD, 1)),
            _resident((QK_HEAD, 1)),
        ],
        out_specs=[
            pl.BlockSpec((1, n_heads, QK_PAD, tm), lambda b, i: (b, 0, 0, i)),
            pl.BlockSpec((1, n_heads, 1, tm, QK_PAD), lambda b, i: (b, 0, i, 0, 0)),
            pl.BlockSpec((1, n_heads, 1, V_HEAD, tm), lambda b, i: (b, 0, i, 0, 0)),
        ],
        out_shape=[
            jax.ShapeDtypeStruct((B, n_heads, QK_PAD, S), BF16),
            jax.ShapeDtypeStruct((B, n_heads, n_t, tm, QK_PAD), BF16),
            jax.ShapeDtypeStruct((B, n_heads, n_t, V_HEAD, tm), BF16),
        ],
        compiler_params=_cparams(("parallel", "parallel")),
        name="mla_pre",
    )(bound, x3, pos3, invf, gmix.reshape(1, D), winT, gq.reshape(-1, 1), wqT, gkv.reshape(-1, 1), wkvT,
      gqh.reshape(-1, 1), gkh.reshape(-1, 1))


def _attn_kernel(safe_ref, qT_ref, k_ref, vT_ref, o_ref, m_ref, l_ref, acc_ref, s0_ref, s1_ref):
    i = pl.program_id(2)
    qT = qT_ref[0, 0]
    sub = l_ref.shape[0]
    l_ref[...] = jnp.zeros(l_ref.shape, F32)
    acc_ref[...] = jnp.zeros(acc_ref.shape, F32)

    def qk(j, s_ref):
        s_ref[...] = jnp.dot(k_ref[0, 0, j], qT, preferred_element_type=F32)

    tk, tq = s0_ref.shape

    def causal(sT, d):
        key = lax.broadcasted_iota(jnp.int32, sT.shape, 0) + d * tk
        qry = lax.broadcasted_iota(jnp.int32, sT.shape, 1)
        return jnp.where(key <= qry, sT, -jnp.inf)

    def partial_sum(p):
        return jnp.sum(p.reshape(p.shape[0] // sub, sub, p.shape[1]), axis=0)

    def exp_pv(j, s_ref, diag=None):
        sT = s_ref[...]
        p = jnp.exp2(sT if diag is None else causal(sT, diag))
        l_ref[...] += partial_sum(p)
        acc_ref[...] += jnp.dot(vT_ref[0, 0, j], p.astype(BF16), preferred_element_type=F32)

    def online(j, diag=None):
        qk(j, s0_ref)
        sT = s0_ref[...]
        sT = sT if diag is None else causal(sT, diag)
        m_old = m_ref[...]
        m_new = jnp.maximum(m_old, jnp.max(sT, axis=0, keepdims=True))
        alpha = jnp.exp2(m_old - m_new)
        p = jnp.exp2(sT - m_new)
        l_ref[...] = alpha * l_ref[...] + partial_sum(p)
        acc_ref[...] = alpha * acc_ref[...] + jnp.dot(vT_ref[0, 0, j], p.astype(BF16),
                                                      preferred_element_type=F32)
        m_ref[...] = m_new

    @pl.when(safe_ref[0] != 0)
    def _():
        qk(0, s0_ref)

        def pair(t, carry):
            qk(2 * t + 1, s1_ref)
            exp_pv(2 * t, s0_ref)
            qk(2 * t + 2, s0_ref)
            exp_pv(2 * t + 1, s1_ref)
            return carry

        lax.fori_loop(0, i, pair, 0)
        qk(2 * i + 1, s1_ref)
        exp_pv(2 * i, s0_ref, diag=0)
        exp_pv(2 * i + 1, s1_ref, diag=1)

    @pl.when(safe_ref[0] == 0)
    def _():
        m_ref[...] = jnp.full(m_ref.shape, -jnp.inf, F32)

        def body(j, carry):
            online(j)
            return carry

        lax.fori_loop(0, 2 * i, body, 0)
        online(2 * i, diag=0)
        online(2 * i + 1, diag=1)

    l = jnp.sum(l_ref[...], axis=0, keepdims=True)
    o_ref[0] = (acc_ref[...] / l).T.astype(o_ref.dtype)


def _attn(safe, qT, k, vT):
    B, H, _, S = qT.shape
    n_k, tk = k.shape[2], k.shape[3]
    tq = 2 * tk
    assert S % tq == 0, (S, tq)
    return pl.pallas_call(
        _attn_kernel,
        grid=(B, H, S // tq),
        in_specs=[
            pl.BlockSpec(memory_space=pltpu.SMEM),
            pl.BlockSpec((1, 1, QK_PAD, tq), lambda b, h, i: (b, h, 0, i)),
            pl.BlockSpec((1, 1, n_k, tk, QK_PAD), lambda b, h, i: (b, h, 0, 0, 0)),
            pl.BlockSpec((1, 1, n_k, V_HEAD, tk), lambda b, h, i: (b, h, 0, 0, 0)),
        ],
        out_specs=pl.BlockSpec((1, tq, V_HEAD), lambda b, h, i: (b, i, h)),
        out_shape=jax.ShapeDtypeStruct((B, S, H * V_HEAD), BF16),
        scratch_shapes=[pltpu.VMEM((1, tq), F32), pltpu.VMEM((SUBLANES, tq), F32),
                        pltpu.VMEM((V_HEAD, tq), F32), pltpu.VMEM((tk, tq), F32), pltpu.VMEM((tk, tq), F32)],
        compiler_params=_cparams(("parallel", "parallel", "arbitrary")),
        name="attn",
    )(safe, qT, k, vT)


def _out_proj_kernel(x_ref, o_ref, w_ref, y_ref):
    y_ref[...] = x_ref[...] + jnp.dot(o_ref[...], w_ref[...], preferred_element_type=F32)


def _out_proj(x2, o2, w, tm):
    T, D = x2.shape
    return pl.pallas_call(
        _out_proj_kernel,
        grid=(T // tm,),
        in_specs=[
            pl.BlockSpec((tm, D), lambda i: (i, 0)),
            pl.BlockSpec((tm, o2.shape[1]), lambda i: (i, 0)),
            _resident(w.shape),
        ],
        out_specs=pl.BlockSpec((tm, D), lambda i: (i, 0)),
        out_shape=jax.ShapeDtypeStruct((T, D), F32),
        compiler_params=_cparams(("parallel",)),
        name="out_proj",
    )(x2, o2, w)


def _tile(n, want):
    t = min(n, want)
    assert n % t == 0, (n, t)
    return t


def kernel(x, positions, ffn1_norm, ffn1_w_gate, ffn1_w_up, ffn1_w_down, mix_norm, pool_w, pool_scale,
           mla_w_in, mla_q_norm, mla_w_q_up, mla_kv_norm, mla_w_kv_up, mla_q_head_norm, mla_k_head_norm,
           mla_w_out, ffn2_norm, ffn2_w_gate, ffn2_w_up, ffn2_w_down):
    B, S, D = x.shape
    depth = ffn1_norm.shape[0]
    n_heads = mla_w_out.shape[1] // V_HEAD
    T = B * S
    tm_ffn = _tile(T, 512)
    tm_pool = _tile(S, 512)
    tk = _tile(S // 2, 512)
    tm_out = _tile(T, 1024)

    invf = (1.0 / (ROPE_THETA ** (jnp.arange(0, QK_ROPE, 2, dtype=F32) / QK_ROPE))).reshape(-1, 1)
    pos3 = positions.reshape(B, 1, S)

    for i in range(depth):
        x = _ffn(x.reshape(T, D), ffn1_norm[i], ffn1_w_gate[i].astype(BF16), ffn1_w_up[i].astype(BF16),
                 ffn1_w_down[i].astype(BF16), tm_ffn).reshape(B, S, D)
        j = i // 2
        if i % 2 == 0:
            x = _pool(x, mix_norm[i], pool_w[j].astype(BF16), pool_scale[j], tm_pool)
        else:
            bound = (Q_SCALE * QK_HEAD * jnp.max(jnp.abs(mla_q_head_norm[j]))
                     * jnp.max(jnp.abs(mla_k_head_norm[j]))).reshape(1)
            safe = (2.0 * bound <= F32_SAFE_EXP2).astype(jnp.int32)
            qT, k, vT = _mla_pre(bound, x, pos3, invf, mix_norm[i], mla_w_in[j].T.astype(BF16), mla_q_norm[j],
                                 mla_w_q_up[j].T.astype(BF16), mla_kv_norm[j], mla_w_kv_up[j].T.astype(BF16),
                                 mla_q_head_norm[j], mla_k_head_norm[j], n_heads, tk)
            o = _attn(safe, qT, k, vT)
            x = _out_proj(x.reshape(T, D), o.reshape(T, n_heads * V_HEAD), mla_w_out[j].astype(BF16),
                          tm_out).reshape(B, S, D)
        x = _ffn(x.reshape(T, D), ffn2_norm[i], ffn2_w_gate[i].astype(BF16), ffn2_w_up[i].astype(BF16),
                 ffn2_w_down[i].astype(BF16), tm_ffn).reshape(B, S, D)
    return x
```

```python
import functools
import math

import jax
import jax.numpy as jnp
from jax import lax
from jax.experimental import pallas as pl
from jax.experimental.pallas import tpu as pltpu

F32 = jnp.float32
BF16 = jnp.bfloat16

EPS = 1e-6
FFN_HALF = 0.5
POOL_WINDOWS = (2, 4, 8, 16)
POOL_HALO = 16
QK_NOPE = 128
QK_ROPE = 64
QK_HEAD = QK_NOPE + QK_ROPE
QK_PAD = 256
V_HEAD = 128
ROPE_THETA = 10000.0
LOG2E = math.log2(math.e)
Q_SCALE = QK_HEAD ** -0.5 * LOG2E
SUBLANES = 8
F32_SAFE_EXP2 = 120.0

VMEM_LIMIT = 56 * 1024 * 1024


def _cparams(sem):
    return pltpu.CompilerParams(dimension_semantics=sem, vmem_limit_bytes=VMEM_LIMIT)


def _resident(shape):
    nd = len(shape)
    return pl.BlockSpec(shape, lambda *_: (0,) * nd, pipeline_mode=pl.Buffered(1))


def _rms_rows(x, gain):
    ms = jnp.mean(x * x, axis=-1, keepdims=True)
    return x * lax.rsqrt(ms + EPS) * gain


def _swiglu_residual(x, g_ref, wg_ref, wu_ref, wd_ref):
    hn = _rms_rows(x, g_ref[...]).astype(BF16)
    g = jnp.dot(hn, wg_ref[...], preferred_element_type=F32)
    u = jnp.dot(hn, wu_ref[...], preferred_element_type=F32)
    a = (g / (1.0 + jnp.exp(-g)) * u).astype(BF16)
    y = jnp.dot(a, wd_ref[...], preferred_element_type=F32)
    return x + FFN_HALF * y


def _ffn_kernel(x_ref, g_ref, wg_ref, wu_ref, wd_ref, o_ref):
    o_ref[...] = _swiglu_residual(x_ref[...], g_ref, wg_ref, wu_ref, wd_ref)


def _proj_ffn_kernel(x_ref, a_ref, wo_ref, g_ref, wg_ref, wu_ref, wd_ref, o_ref):
    x = x_ref[...] + jnp.dot(a_ref[...], wo_ref[...], preferred_element_type=F32)
    o_ref[...] = _swiglu_residual(x, g_ref, wg_ref, wu_ref, wd_ref)


def _ffn(x2, gain, wg, wu, wd, tm, attn_out=None, w_out=None):
    T, D = x2.shape
    F = wg.shape[1]
    row_tile = pl.BlockSpec((tm, D), lambda i: (i, 0))
    ffn_specs = [_resident((1, D)), _resident((D, F)), _resident((D, F)), _resident((F, D))]
    ffn_args = (gain.reshape(1, D), wg, wu, wd)
    if attn_out is None:
        body, specs, args = _ffn_kernel, [row_tile] + ffn_specs, (x2,) + ffn_args
    else:
        body = _proj_ffn_kernel
        specs = [row_tile, pl.BlockSpec((tm, attn_out.shape[1]), lambda i: (i, 0)), _resident(w_out.shape)] + ffn_specs
        args = (x2, attn_out, w_out) + ffn_args
    return pl.pallas_call(
        body,
        grid=(T // tm,),
        in_specs=specs,
        out_specs=row_tile,
        out_shape=jax.ShapeDtypeStruct((T, D), F32),
        compiler_params=_cparams(("parallel",)),
        name="ffn" if attn_out is None else "proj_ffn",
    )(*args)


def _pool_kernel(x_ref, halo_ref, g_ref, w_ref, sc_ref, o_ref, hbuf, *, tm):
    i = pl.program_id(1)
    gain = g_ref[...]
    x = x_ref[0]
    hn = _rms_rows(x, gain)
    hh = _rms_rows(halo_ref[0], gain)
    hbuf[0:POOL_HALO] = jnp.where(i > 0, hh, 0.0)
    hbuf[POOL_HALO:POOL_HALO + tm] = hn
    t = i * tm + lax.broadcasted_iota(jnp.int32, (tm, 1), 0)
    C = w_ref.shape[1]
    for g, w_len in enumerate(POOL_WINDOWS):
        cols = slice(g * C, (g + 1) * C)
        u = hn[:, cols]
        wsum = u
        for j in range(1, w_len):
            wsum = wsum + hbuf[POOL_HALO - j:POOL_HALO - j + tm, cols]
        count = jnp.minimum(t + 1, w_len).astype(F32)
        pooled = (wsum / count - u).astype(BF16)
        y = jnp.dot(pooled, w_ref[g], preferred_element_type=F32)
        o_ref[0, :, cols] = x[:, cols] + y * sc_ref[:, cols]


def _pool(x3, gain, w, scale, tm):
    B, S, D = x3.shape
    G, C, _ = w.shape
    hb = tm // POOL_HALO
    return pl.pallas_call(
        functools.partial(_pool_kernel, tm=tm),
        grid=(B, S // tm),
        in_specs=[
            pl.BlockSpec((1, tm, D), lambda b, i: (b, i, 0)),
            pl.BlockSpec((1, POOL_HALO, D), lambda b, i: (b, jnp.maximum(i * hb - 1, 0), 0)),
            _resident((1, D)),
            _resident((G, C, C)),
            _resident((1, D)),
        ],
        out_specs=pl.BlockSpec((1, tm, D), lambda b, i: (b, i, 0)),
        out_shape=jax.ShapeDtypeStruct((B, S, D), F32),
        scratch_shapes=[pltpu.VMEM((POOL_HALO + tm, D), F32)],
        compiler_params=_cparams(("parallel", "parallel")),
        name="pool",
    )(x3, x3, gain.reshape(1, D), w, scale.reshape(1, D))


def _rms_cols(xT, gain_col, n):
    ms = jnp.sum(xT * xT, axis=0, keepdims=True) * (1.0 / n)
    return xT * lax.rsqrt(ms + EPS) * gain_col


def _mla_pre_kernel(bound_ref, x_ref, pos_ref, invf_ref, gmix_ref, winT_ref, gq_ref, wqT_ref, gkv_ref, wkvT_ref,
                    gqh_ref, gkh_ref, qT_ref, k_ref, vT_ref, *, n_heads, q_lora, kv_lora, q_scale):
    tm = x_ref.shape[1]
    hn = _rms_rows(x_ref[0], gmix_ref[...]).astype(BF16)
    latT = lax.dot_general(winT_ref[...], hn, (((1,), (1,)), ((), ())), preferred_element_type=F32)
    cqT = latT[:q_lora]
    ckvT = latT[q_lora:q_lora + kv_lora]
    kpeT = latT[q_lora + kv_lora:]
    qT = jnp.dot(wqT_ref[...], _rms_cols(cqT, gq_ref[...], q_lora).astype(BF16),
                 preferred_element_type=F32)
    kvT = jnp.dot(wkvT_ref[...], _rms_cols(ckvT, gkv_ref[...], kv_lora).astype(BF16),
                  preferred_element_type=F32)

    ang = pos_ref[0].astype(F32) * invf_ref[...]
    cos, sin = jnp.cos(ang), jnp.sin(ang)
    half = QK_ROPE // 2
    gqh = gqh_ref[...]
    gkh = gkh_ref[...]
    kpe_ss = jnp.sum(kpeT * kpeT, axis=0, keepdims=True)
    first = lax.broadcasted_iota(jnp.int32, (QK_PAD - QK_HEAD, tm), 0) == 0
    q_pad = jnp.where(first, 1.0, 0.0)
    k_pad = jnp.where(first, -bound_ref[0], 0.0)

    def rope(x1, x2):
        return x1 * cos - x2 * sin, x2 * cos + x1 * sin

    for h in range(n_heads):
        qh = qT[h * QK_HEAD:(h + 1) * QK_HEAD]
        ms = jnp.sum(qh * qh, axis=0, keepdims=True) * (1.0 / QK_HEAD)
        qn = qh * (lax.rsqrt(ms + EPS) * q_scale) * gqh
        r1, r2 = rope(qn[QK_NOPE:QK_NOPE + half], qn[QK_NOPE + half:])
        qT_ref[0, h] = jnp.concatenate([qn[:QK_NOPE], r1, r2, q_pad], axis=0).astype(BF16)

        kn = kvT[h * (QK_NOPE + V_HEAD):h * (QK_NOPE + V_HEAD) + QK_NOPE]
        v = kvT[h * (QK_NOPE + V_HEAD) + QK_NOPE:(h + 1) * (QK_NOPE + V_HEAD)]
        ms = (jnp.sum(kn * kn, axis=0, keepdims=True) + kpe_ss) * (1.0 / QK_HEAD)
        rs = lax.rsqrt(ms + EPS)
        r1, r2 = rope(kpeT[:half] * rs * gkh[QK_NOPE:QK_NOPE + half],
                      kpeT[half:] * rs * gkh[QK_NOPE + half:])
        k_ref[0, h, 0, :, :QK_NOPE] = (kn * rs * gkh[:QK_NOPE]).T.astype(BF16)
        k_ref[0, h, 0, :, QK_NOPE:] = jnp.concatenate([r1, r2, k_pad], axis=0).T.astype(BF16)
        vT_ref[0, h, 0] = v.astype(BF16)


def _mla_pre(bound, x3, pos3, invf, gmix, winT, gq, wqT, gkv, wkvT, gqh, gkh, n_heads, tm):
    B, S, D = x3.shape
    q_lora, kv_lora = wqT.shape[1], wkvT.shape[1]
    n_t = S // tm
    kern = functools.partial(_mla_pre_kernel, n_heads=n_heads, q_lora=q_lora, kv_lora=kv_lora,
                             q_scale=Q_SCALE)
    return pl.pallas_call(
        kern,
        grid=(B, n_t),
        in_specs=[
            pl.BlockSpec(memory_space=pltpu.SMEM),
            pl.BlockSpec((1, tm, D), lambda b, i: (b, i, 0)),
            pl.BlockSpec((1, 1, tm), lambda b, i: (b, 0, i)),
            _resident(invf.shape),
            _resident((1, D)),
            _resident(winT.shape),
            _resident((q_lora, 1)),
            _resident(wqT.shape),
            _resident((kv_lora, 1)),
            _resident(wkvT.shape),
            _resident((QK_HEAD, 1)),
            _resident((QK_HEAD, 1)),
        ],
        out_specs=[
            pl.BlockSpec((1, n_heads, QK_PAD, tm), lambda b, i: (b, 0, 0, i)),
            pl.BlockSpec((1, n_heads, 1, tm, QK_PAD), lambda b, i: (b, 0, i, 0, 0)),
            pl.BlockSpec((1, n_heads, 1, V_HEAD, tm), lambda b, i: (b, 0, i, 0, 0)),
        ],
        out_shape=[
            jax.ShapeDtypeStruct((B, n_heads, QK_PAD, S), BF16),
            jax.ShapeDtypeStruct((B, n_heads, n_t, tm, QK_PAD), BF16),
            jax.ShapeDtypeStruct((B, n_heads, n_t, V_HEAD, tm), BF16),
        ],
        compiler_params=_cparams(("parallel", "parallel")),
        name="mla_pre",
    )(bound, x3, pos3, invf, gmix.reshape(1, D), winT, gq.reshape(-1, 1), wqT, gkv.reshape(-1, 1), wkvT,
      gqh.reshape(-1, 1), gkh.reshape(-1, 1))


def _attn_kernel(safe_ref, qT_ref, k_ref, vT_ref, o_ref, m_ref, l_ref, acc_ref, s0_ref, s1_ref):
    i = pl.program_id(2)
    sub = l_ref.shape[0]
    l_ref[...] = jnp.zeros(l_ref.shape, F32)
    acc_ref[...] = jnp.zeros(acc_ref.shape, F32)

    def qk(j, s_ref):
        s_ref[...] = jnp.dot(k_ref[0, 0, j], qT_ref[0, 0], preferred_element_type=F32)

    tk, tq = s0_ref.shape

    def causal(sT, d):
        key = lax.broadcasted_iota(jnp.int32, sT.shape, 0) + d * tk
        qry = lax.broadcasted_iota(jnp.int32, sT.shape, 1)
        return jnp.where(key <= qry, sT, -jnp.inf)

    def partial_sum(p):
        return jnp.sum(p.reshape(p.shape[0] // sub, sub, p.shape[1]), axis=0)

    def exp_pv(j, s_ref, diag=None):
        sT = s_ref[...]
        p = jnp.exp2(sT if diag is None else causal(sT, diag))
        l_ref[...] += partial_sum(p)
        acc_ref[...] += jnp.dot(vT_ref[0, 0, j], p.astype(BF16), preferred_element_type=F32)

    def online(j, diag=None):
        qk(j, s0_ref)
        sT = s0_ref[...]
        sT = sT if diag is None else causal(sT, diag)
        m_old = m_ref[...]
        m_new = jnp.maximum(m_old, jnp.max(sT, axis=0, keepdims=True))
        alpha = jnp.exp2(m_old - m_new)
        p = jnp.exp2(sT - m_new)
        l_ref[...] = alpha * l_ref[...] + partial_sum(p)
        acc_ref[...] = alpha * acc_ref[...] + jnp.dot(vT_ref[0, 0, j], p.astype(BF16),
                                                      preferred_element_type=F32)
        m_ref[...] = m_new

    @pl.when(safe_ref[0] != 0)
    def _():
        qk(0, s0_ref)

        def pair(t):
            qk(2 * t + 1, s1_ref)
            exp_pv(2 * t, s0_ref)
            qk(2 * t + 2, s0_ref)
            exp_pv(2 * t + 1, s1_ref)

        def two_pairs(t, carry):
            pair(2 * t)
            pair(2 * t + 1)
            return carry

        lax.fori_loop(0, i // 2, two_pairs, 0)

        @pl.when(i % 2 == 1)
        def _():
            pair(i - 1)

        qk(2 * i + 1, s1_ref)
        exp_pv(2 * i, s0_ref, diag=0)
        exp_pv(2 * i + 1, s1_ref, diag=1)

    @pl.when(safe_ref[0] == 0)
    def _():
        m_ref[...] = jnp.full(m_ref.shape, -jnp.inf, F32)

        def body(j, carry):
            online(j)
            return carry

        lax.fori_loop(0, 2 * i, body, 0)
        online(2 * i, diag=0)
        online(2 * i + 1, diag=1)

    l = jnp.sum(l_ref[...], axis=0, keepdims=True)
    o_ref[0] = (acc_ref[...] / l).T.astype(o_ref.dtype)


def _attn(safe, qT, k, vT):
    B, H, _, S = qT.shape
    n_k, tk = k.shape[2], k.shape[3]
    tq = 2 * tk
    assert S % tq == 0, (S, tq)
    return pl.pallas_call(
        _attn_kernel,
        grid=(B, H, S // tq),
        in_specs=[
            pl.BlockSpec(memory_space=pltpu.SMEM),
            pl.BlockSpec((1, 1, QK_PAD, tq), lambda b, h, i: (b, h, 0, i)),
            pl.BlockSpec((1, 1, n_k, tk, QK_PAD), lambda b, h, i: (b, h, 0, 0, 0)),
            pl.BlockSpec((1, 1, n_k, V_HEAD, tk), lambda b, h, i: (b, h, 0, 0, 0)),
        ],
        out_specs=pl.BlockSpec((1, tq, V_HEAD), lambda b, h, i: (b, i, h)),
        out_shape=jax.ShapeDtypeStruct((B, S, H * V_HEAD), BF16),
        scratch_shapes=[pltpu.VMEM((1, tq), F32), pltpu.VMEM((SUBLANES, tq), F32),
                        pltpu.VMEM((V_HEAD, tq), F32), pltpu.VMEM((tk, tq), F32), pltpu.VMEM((tk, tq), F32)],
        compiler_params=_cparams(("parallel", "parallel", "arbitrary")),
        name="attn",
    )(safe, qT, k, vT)


def _tile(n, want):
    t = min(n, want)
    assert n % t == 0, (n, t)
    return t


def kernel(x, positions, ffn1_norm, ffn1_w_gate, ffn1_w_up, ffn1_w_down, mix_norm, pool_w, pool_scale,
           mla_w_in, mla_q_norm, mla_w_q_up, mla_kv_norm, mla_w_kv_up, mla_q_head_norm, mla_k_head_norm,
           mla_w_out, ffn2_norm, ffn2_w_gate, ffn2_w_up, ffn2_w_down):
    B, S, D = x.shape
    depth = ffn1_norm.shape[0]
    n_heads = mla_w_out.shape[1] // V_HEAD
    T = B * S
    tm_ffn = _tile(T, 512)
    tm_pool = _tile(S, 512)
    tk = _tile(S // 2, 512)

    invf = (1.0 / (ROPE_THETA ** (jnp.arange(0, QK_ROPE, 2, dtype=F32) / QK_ROPE))).reshape(-1, 1)
    pos3 = positions.reshape(B, 1, S)

    for i in range(depth):
        x = _ffn(x.reshape(T, D), ffn1_norm[i], ffn1_w_gate[i].astype(BF16), ffn1_w_up[i].astype(BF16),
                 ffn1_w_down[i].astype(BF16), tm_ffn).reshape(B, S, D)
        j = i // 2
        proj = {}
        if i % 2 == 0:
            x = _pool(x, mix_norm[i], pool_w[j].astype(BF16), pool_scale[j], tm_pool)
        else:
            bound = (Q_SCALE * QK_HEAD * jnp.max(jnp.abs(mla_q_head_norm[j]))
                     * jnp.max(jnp.abs(mla_k_head_norm[j]))).reshape(1)
            safe = (2.0 * bound <= F32_SAFE_EXP2).astype(jnp.int32)
            qT, k, vT = _mla_pre(bound, x, pos3, invf, mix_norm[i], mla_w_in[j].T.astype(BF16), mla_q_norm[j],
                                 mla_w_q_up[j].T.astype(BF16), mla_kv_norm[j], mla_w_kv_up[j].T.astype(BF16),
                                 mla_q_head_norm[j], mla_k_head_norm[j], n_heads, tk)
            o = _attn(safe, qT, k, vT)
            proj = dict(attn_out=o.reshape(T, n_heads * V_HEAD), w_out=mla_w_out[j].astype(BF16))
        x = _ffn(x.reshape(T, D), ffn2_norm[i], ffn2_w_gate[i].astype(BF16), ffn2_w_up[i].astype(BF16),
                 ffn2_w_down[i].astype(BF16), tm_ffn, **proj).reshape(B, S, D)
    return x
```

```python
import functools
import math

import jax
import jax.numpy as jnp
from jax import lax
from jax.experimental import pallas as pl
from jax.experimental.pallas import tpu as pltpu

F32 = jnp.float32
BF16 = jnp.bfloat16

EPS = 1e-6
FFN_HALF = 0.5
POOL_WINDOWS = (2, 4, 8, 16)
POOL_HALO = 16
QK_NOPE = 128
QK_ROPE = 64
QK_HEAD = QK_NOPE + QK_ROPE
QK_PAD = 256
V_HEAD = 128
ROPE_THETA = 10000.0
LOG2E = math.log2(math.e)
Q_SCALE = QK_HEAD ** -0.5 * LOG2E
SUBLANES = 8
F32_SAFE_EXP2 = 120.0

VMEM_LIMIT = 56 * 1024 * 1024


def _cparams(sem):
    return pltpu.CompilerParams(dimension_semantics=sem, vmem_limit_bytes=VMEM_LIMIT)


def _resident(shape):
    nd = len(shape)
    return pl.BlockSpec(shape, lambda *_: (0,) * nd, pipeline_mode=pl.Buffered(1))


def _rms_rows(x, gain):
    ms = jnp.mean(x * x, axis=-1, keepdims=True)
    return x * lax.rsqrt(ms + EPS) * gain


def _swiglu_residual(x, g_ref, wg_ref, wu_ref, wd_ref):
    hn = _rms_rows(x, g_ref[...]).astype(BF16)
    g = jnp.dot(hn, wg_ref[...], preferred_element_type=F32)
    u = jnp.dot(hn, wu_ref[...], preferred_element_type=F32)
    a = (g / (1.0 + jnp.exp(-g)) * u).astype(BF16)
    y = jnp.dot(a, wd_ref[...], preferred_element_type=F32)
    return x + FFN_HALF * y


def _ffn_kernel(x_ref, g_ref, wg_ref, wu_ref, wd_ref, o_ref):
    o_ref[...] = _swiglu_residual(x_ref[...], g_ref, wg_ref, wu_ref, wd_ref)


def _proj_ffn_kernel(x_ref, a_ref, wo_ref, g_ref, wg_ref, wu_ref, wd_ref, o_ref):
    x = x_ref[...] + jnp.dot(a_ref[...], wo_ref[...], preferred_element_type=F32)
    o_ref[...] = _swiglu_residual(x, g_ref, wg_ref, wu_ref, wd_ref)


def _ffn(x2, gain, wg, wu, wd, tm, attn_out=None, w_out=None):
    T, D = x2.shape
    F = wg.shape[1]
    row_tile = pl.BlockSpec((tm, D), lambda i: (i, 0))
    ffn_specs = [_resident((1, D)), _resident((D, F)), _resident((D, F)), _resident((F, D))]
    ffn_args = (gain.reshape(1, D), wg, wu, wd)
    if attn_out is None:
        body, specs, args = _ffn_kernel, [row_tile] + ffn_specs, (x2,) + ffn_args
    else:
        body = _proj_ffn_kernel
        specs = [row_tile, pl.BlockSpec((tm, attn_out.shape[1]), lambda i: (i, 0)), _resident(w_out.shape)] + ffn_specs
        args = (x2, attn_out, w_out) + ffn_args
    return pl.pallas_call(
        body,
        grid=(T // tm,),
        in_specs=specs,
        out_specs=row_tile,
        out_shape=jax.ShapeDtypeStruct((T, D), F32),
        compiler_params=_cparams(("parallel",)),
        name="ffn" if attn_out is None else "proj_ffn",
    )(*args)


def _pool_kernel(x_ref, halo_ref, g_ref, w_ref, sc_ref, o_ref, hbuf, *, tm):
    i = pl.program_id(1)
    gain = g_ref[...]
    x = x_ref[0]
    hn = _rms_rows(x, gain)
    hh = _rms_rows(halo_ref[0], gain)
    hbuf[0:POOL_HALO] = jnp.where(i > 0, hh, 0.0)
    hbuf[POOL_HALO:POOL_HALO + tm] = hn
    t = i * tm + lax.broadcasted_iota(jnp.int32, (tm, 1), 0)
    C = w_ref.shape[1]
    for g, w_len in enumerate(POOL_WINDOWS):
        cols = slice(g * C, (g + 1) * C)
        u = hn[:, cols]
        wsum = u
        for j in range(1, w_len):
            wsum = wsum + hbuf[POOL_HALO - j:POOL_HALO - j + tm, cols]
        count = jnp.minimum(t + 1, w_len).astype(F32)
        pooled = (wsum / count - u).astype(BF16)
        y = jnp.dot(pooled, w_ref[g], preferred_element_type=F32)
        o_ref[0, :, cols] = x[:, cols] + y * sc_ref[:, cols]


def _pool(x3, gain, w, scale, tm):
    B, S, D = x3.shape
    G, C, _ = w.shape
    hb = tm // POOL_HALO
    return pl.pallas_call(
        functools.partial(_pool_kernel, tm=tm),
        grid=(B, S // tm),
        in_specs=[
            pl.BlockSpec((1, tm, D), lambda b, i: (b, i, 0)),
            pl.BlockSpec((1, POOL_HALO, D), lambda b, i: (b, jnp.maximum(i * hb - 1, 0), 0)),
            _resident((1, D)),
            _resident((G, C, C)),
            _resident((1, D)),
        ],
        out_specs=pl.BlockSpec((1, tm, D), lambda b, i: (b, i, 0)),
        out_shape=jax.ShapeDtypeStruct((B, S, D), F32),
        scratch_shapes=[pltpu.VMEM((POOL_HALO + tm, D), F32)],
        compiler_params=_cparams(("parallel", "parallel")),
        name="pool",
    )(x3, x3, gain.reshape(1, D), w, scale.reshape(1, D))


def _rms_cols(xT, gain_col, n):
    ms = jnp.sum(xT * xT, axis=0, keepdims=True) * (1.0 / n)
    return xT * lax.rsqrt(ms + EPS) * gain_col


def _mla_pre_kernel(bound_ref, x_ref, pos_ref, invf_ref, gmix_ref, winT_ref, gq_ref, wqT_ref, gkv_ref, wkvT_ref,
                    gqh_ref, gkh_ref, qT_ref, k_ref, vT_ref, *, n_heads, q_lora, kv_lora, q_scale):
    tm = x_ref.shape[1]
    hn = _rms_rows(x_ref[0], gmix_ref[...]).astype(BF16)
    latT = lax.dot_general(winT_ref[...], hn, (((1,), (1,)), ((), ())), preferred_element_type=F32)
    cqT = latT[:q_lora]
    ckvT = latT[q_lora:q_lora + kv_lora]
    kpeT = latT[q_lora + kv_lora:]
    qT = jnp.dot(wqT_ref[...], _rms_cols(cqT, gq_ref[...], q_lora).astype(BF16),
                 preferred_element_type=F32)
    kvT = jnp.dot(wkvT_ref[...], _rms_cols(ckvT, gkv_ref[...], kv_lora).astype(BF16),
                  preferred_element_type=F32)

    ang = pos_ref[0].astype(F32) * invf_ref[...]
    cos, sin = jnp.cos(ang), jnp.sin(ang)
    half = QK_ROPE // 2
    gqh = gqh_ref[...]
    gkh = gkh_ref[...]
    kpe_ss = jnp.sum(kpeT * kpeT, axis=0, keepdims=True)
    first = lax.broadcasted_iota(jnp.int32, (QK_PAD - QK_HEAD, tm), 0) == 0
    q_pad = jnp.where(first, 1.0, 0.0)
    k_pad = jnp.where(first, -bound_ref[0], 0.0)

    def rope(x1, x2):
        return x1 * cos - x2 * sin, x2 * cos + x1 * sin

    for h in range(n_heads):
        qh = qT[h * QK_HEAD:(h + 1) * QK_HEAD]
        ms = jnp.sum(qh * qh, axis=0, keepdims=True) * (1.0 / QK_HEAD)
        qn = qh * (lax.rsqrt(ms + EPS) * q_scale) * gqh
        r1, r2 = rope(qn[QK_NOPE:QK_NOPE + half], qn[QK_NOPE + half:])
        qT_ref[0, h] = jnp.concatenate([qn[:QK_NOPE], r1, r2, q_pad], axis=0).astype(BF16)

        kn = kvT[h * (QK_NOPE + V_HEAD):h * (QK_NOPE + V_HEAD) + QK_NOPE]
        v = kvT[h * (QK_NOPE + V_HEAD) + QK_NOPE:(h + 1) * (QK_NOPE + V_HEAD)]
        ms = (jnp.sum(kn * kn, axis=0, keepdims=True) + kpe_ss) * (1.0 / QK_HEAD)
        rs = lax.rsqrt(ms + EPS)
        r1, r2 = rope(kpeT[:half] * rs * gkh[QK_NOPE:QK_NOPE + half],
                      kpeT[half:] * rs * gkh[QK_NOPE + half:])
        k_ref[0, h, 0, :, :QK_NOPE] = (kn * rs * gkh[:QK_NOPE]).T.astype(BF16)
        k_ref[0, h, 0, :, QK_NOPE:] = jnp.concatenate([r1, r2, k_pad], axis=0).T.astype(BF16)
        vT_ref[0, h, 0] = v.astype(BF16)


def _mla_pre(bound, x3, pos3, invf, gmix, winT, gq, wqT, gkv, wkvT, gqh, gkh, n_heads, tm):
    B, S, D = x3.shape
    q_lora, kv_lora = wqT.shape[1], wkvT.shape[1]
    n_t = S // tm
    kern = functools.partial(_mla_pre_kernel, n_heads=n_heads, q_lora=q_lora, kv_lora=kv_lora,
                             q_scale=Q_SCALE)
    return pl.pallas_call(
        kern,
        grid=(B, n_t),
        in_specs=[
            pl.BlockSpec(memory_space=pltpu.SMEM),
            pl.BlockSpec((1, tm, D), lambda b, i: (b, i, 0)),
            pl.BlockSpec((1, 1, tm), lambda b, i: (b, 0, i)),
            _resident(invf.shape),
            _resident((1, D)),
            _resident(winT.shape),
            _resident((q_lora, 1)),
            _resident(wqT.shape),
            _resident((kv_lora, 1)),
            _resident(wkvT.shape),
            _resident((QK_HEAD, 1)),
            _resident((QK_HEAD, 1)),
        ],
        out_specs=[
            pl.BlockSpec((1, n_heads, QK_PAD, tm), lambda b, i: (b, 0, 0, i)),
            pl.BlockSpec((1, n_heads, 1, tm, QK_PAD), lambda b, i: (b, 0, i, 0, 0)),
            pl.BlockSpec((1, n_heads, 1, V_HEAD, tm), lambda b, i: (b, 0, i, 0, 0)),
        ],
        out_shape=[
            jax.ShapeDtypeStruct((B, n_heads, QK_PAD, S), BF16),
            jax.ShapeDtypeStruct((B, n_heads, n_t, tm, QK_PAD), BF16),
            jax.ShapeDtypeStruct((B, n_heads, n_t, V_HEAD, tm), BF16),
        ],
        compiler_params=_cparams(("parallel", "parallel")),
        name="mla_pre",
    )(bound, x3, pos3, invf, gmix.reshape(1, D), winT, gq.reshape(-1, 1), wqT, gkv.reshape(-1, 1), wkvT,
      gqh.reshape(-1, 1), gkh.reshape(-1, 1))


def _attn_kernel(safe_ref, qT_ref, k_ref, vT_ref, o_ref, m_ref, l_ref, acc_ref, p0_ref, p1_ref):
    i = pl.program_id(2)
    sub = l_ref.shape[0]
    tk, tq = p0_ref.shape
    l_ref[...] = jnp.zeros(l_ref.shape, F32)
    acc_ref[...] = jnp.zeros(acc_ref.shape, F32)

    def scores(j, diag=None):
        sT = jnp.dot(k_ref[0, 0, j], qT_ref[0, 0], preferred_element_type=F32)
        if diag is None:
            return sT
        key = lax.broadcasted_iota(jnp.int32, sT.shape, 0) + diag * tk
        qry = lax.broadcasted_iota(jnp.int32, sT.shape, 1)
        return jnp.where(key <= qry, sT, -jnp.inf)

    def partial_sum(p):
        return jnp.sum(p.reshape(p.shape[0] // sub, sub, p.shape[1]), axis=0)

    def probs(j, p_ref, diag=None):
        p = jnp.exp2(scores(j, diag))
        l_ref[...] += partial_sum(p)
        p_ref[...] = p.astype(BF16)

    def pv(j, p_ref):
        acc_ref[...] += jnp.dot(vT_ref[0, 0, j], p_ref[...], preferred_element_type=F32)

    def online(j, diag=None):
        sT = scores(j, diag)
        m_old = m_ref[...]
        m_new = jnp.maximum(m_old, jnp.max(sT, axis=0, keepdims=True))
        alpha = jnp.exp2(m_old - m_new)
        p = jnp.exp2(sT - m_new)
        l_ref[...] = alpha * l_ref[...] + partial_sum(p)
        acc_ref[...] = alpha * acc_ref[...] + jnp.dot(vT_ref[0, 0, j], p.astype(BF16),
                                                      preferred_element_type=F32)
        m_ref[...] = m_new

    @pl.when(safe_ref[0] != 0)
    def _():
        @pl.when(i > 0)
        def _():
            probs(0, p0_ref)

            def pair(t):
                probs(2 * t + 1, p1_ref)
                pv(2 * t, p0_ref)
                probs(2 * t + 2, p0_ref)
                pv(2 * t + 1, p1_ref)

            def two_pairs(t, carry):
                pair(2 * t)
                pair(2 * t + 1)
                return carry

            n = i - 1
            lax.fori_loop(0, n // 2, two_pairs, 0)

            @pl.when(n % 2 == 1)
            def _():
                pair(n - 1)

            probs(2 * i - 1, p1_ref)
            pv(2 * i - 2, p0_ref)
            pv(2 * i - 1, p1_ref)

        probs(2 * i, p0_ref, diag=0)
        probs(2 * i + 1, p1_ref, diag=1)
        pv(2 * i, p0_ref)
        pv(2 * i + 1, p1_ref)

    @pl.when(safe_ref[0] == 0)
    def _():
        m_ref[...] = jnp.full(m_ref.shape, -jnp.inf, F32)

        def body(j, carry):
            online(j)
            return carry

        lax.fori_loop(0, 2 * i, body, 0)
        online(2 * i, diag=0)
        online(2 * i + 1, diag=1)

    l = jnp.sum(l_ref[...], axis=0, keepdims=True)
    o_ref[0] = (acc_ref[...] / l).T.astype(o_ref.dtype)


def _attn(safe, qT, k, vT):
    B, H, _, S = qT.shape
    n_k, tk = k.shape[2], k.shape[3]
    tq = 2 * tk
    assert S % tq == 0, (S, tq)
    return pl.pallas_call(
        _attn_kernel,
        grid=(B, H, S // tq),
        in_specs=[
            pl.BlockSpec(memory_space=pltpu.SMEM),
            pl.BlockSpec((1, 1, QK_PAD, tq), lambda b, h, i: (b, h, 0, i)),
            pl.BlockSpec((1, 1, n_k, tk, QK_PAD), lambda b, h, i: (b, h, 0, 0, 0)),
            pl.BlockSpec((1, 1, n_k, V_HEAD, tk), lambda b, h, i: (b, h, 0, 0, 0)),
        ],
        out_specs=pl.BlockSpec((1, tq, V_HEAD), lambda b, h, i: (b, i, h)),
        out_shape=jax.ShapeDtypeStruct((B, S, H * V_HEAD), BF16),
        scratch_shapes=[pltpu.VMEM((1, tq), F32), pltpu.VMEM((SUBLANES, tq), F32),
                        pltpu.VMEM((V_HEAD, tq), F32), pltpu.VMEM((tk, tq), BF16), pltpu.VMEM((tk, tq), BF16)],
        compiler_params=_cparams(("parallel", "parallel", "arbitrary")),
        name="attn",
    )(safe, qT, k, vT)


def _tile(n, want):
    t = min(n, want)
    assert n % t == 0, (n, t)
    return t


def kernel(x, positions, ffn1_norm, ffn1_w_gate, ffn1_w_up, ffn1_w_down, mix_norm, pool_w, pool_scale,
           mla_w_in, mla_q_norm, mla_w_q_up, mla_kv_norm, mla_w_kv_up, mla_q_head_norm, mla_k_head_norm,
           mla_w_out, ffn2_norm, ffn2_w_gate, ffn2_w_up, ffn2_w_down):
    B, S, D = x.shape
    depth = ffn1_norm.shape[0]
    n_heads = mla_w_out.shape[1] // V_HEAD
    T = B * S
    tm_ffn = _tile(T, 512)
    tm_pool = _tile(S, 512)
    tk = _tile(S // 2, 512)

    invf = (1.0 / (ROPE_THETA ** (jnp.arange(0, QK_ROPE, 2, dtype=F32) / QK_ROPE))).reshape(-1, 1)
    pos3 = positions.reshape(B, 1, S)

    for i in range(depth):
        x = _ffn(x.reshape(T, D), ffn1_norm[i], ffn1_w_gate[i].astype(BF16), ffn1_w_up[i].astype(BF16),
                 ffn1_w_down[i].astype(BF16), tm_ffn).reshape(B, S, D)
        j = i // 2
        proj = {}
        if i % 2 == 0:
            x = _pool(x, mix_norm[i], pool_w[j].astype(BF16), pool_scale[j], tm_pool)
        else:
            bound = (Q_SCALE * QK_HEAD * jnp.max(jnp.abs(mla_q_head_norm[j]))
                     * jnp.max(jnp.abs(mla_k_head_norm[j]))).reshape(1)
            safe = (2.0 * bound <= F32_SAFE_EXP2).astype(jnp.int32)
            qT, k, vT = _mla_pre(bound, x, pos3, invf, mix_norm[i], mla_w_in[j].T.astype(BF16), mla_q_norm[j],
                                 mla_w_q_up[j].T.astype(BF16), mla_kv_norm[j], mla_w_kv_up[j].T.astype(BF16),
                                 mla_q_head_norm[j], mla_k_head_norm[j], n_heads, tk)
            o = _attn(safe, qT, k, vT)
            proj = dict(attn_out=o.reshape(T, n_heads * V_HEAD), w_out=mla_w_out[j].astype(BF16))
        x = _ffn(x.reshape(T, D), ffn2_norm[i], ffn2_w_gate[i].astype(BF16), ffn2_w_up[i].astype(BF16),
                 ffn2_w_down[i].astype(BF16), tm_ffn, **proj).reshape(B, S, D)
    return x
```

```python
import functools
import math

import jax
import jax.numpy as jnp
from jax import lax
from jax.experimental import pallas as pl
from jax.experimental.pallas import tpu as pltpu

F32 = jnp.float32
BF16 = jnp.bfloat16

EPS = 1e-6
FFN_HALF = 0.5
POOL_WINDOWS = (2, 4, 8, 16)
POOL_HALO = 16
QK_NOPE = 128
QK_ROPE = 64
QK_HEAD = QK_NOPE + QK_ROPE
QK_PAD = 256
V_HEAD = 128
ROPE_THETA = 10000.0
LOG2E = math.log2(math.e)
Q_SCALE = QK_HEAD ** -0.5 * LOG2E
SUBLANES = 8
F32_SAFE_EXP2 = 120.0

VMEM_LIMIT = 56 * 1024 * 1024


def _cparams(sem):
    return pltpu.CompilerParams(dimension_semantics=sem, vmem_limit_bytes=VMEM_LIMIT)


def _resident(shape):
    nd = len(shape)
    return pl.BlockSpec(shape, lambda *_: (0,) * nd, pipeline_mode=pl.Buffered(1))


def _rms_rows(x, gain):
    ms = jnp.mean(x * x, axis=-1, keepdims=True)
    return x * lax.rsqrt(ms + EPS) * gain


def _swiglu_residual(x, g_ref, wg_ref, wu_ref, wd_ref):
    hn = _rms_rows(x, g_ref[...]).astype(BF16)
    g = jnp.dot(hn, wg_ref[...], preferred_element_type=F32)
    u = jnp.dot(hn, wu_ref[...], preferred_element_type=F32)
    a = (g / (1.0 + jnp.exp(-g)) * u).astype(BF16)
    y = jnp.dot(a, wd_ref[...], preferred_element_type=F32)
    return x + FFN_HALF * y


def _ffn_kernel(x_ref, g_ref, wg_ref, wu_ref, wd_ref, o_ref):
    o_ref[...] = _swiglu_residual(x_ref[...], g_ref, wg_ref, wu_ref, wd_ref)


def _proj_ffn_kernel(x_ref, a_ref, wo_ref, g_ref, wg_ref, wu_ref, wd_ref, o_ref):
    x = x_ref[...] + jnp.dot(a_ref[...], wo_ref[...], preferred_element_type=F32)
    o_ref[...] = _swiglu_residual(x, g_ref, wg_ref, wu_ref, wd_ref)


def _ffn(x2, gain, wg, wu, wd, tm, attn_out=None, w_out=None):
    T, D = x2.shape
    F = wg.shape[1]
    row_tile = pl.BlockSpec((tm, D), lambda i: (i, 0))
    ffn_specs = [_resident((1, D)), _resident((D, F)), _resident((D, F)), _resident((F, D))]
    ffn_args = (gain.reshape(1, D), wg, wu, wd)
    if attn_out is None:
        body, specs, args = _ffn_kernel, [row_tile] + ffn_specs, (x2,) + ffn_args
    else:
        body = _proj_ffn_kernel
        specs = [row_tile, pl.BlockSpec((tm, attn_out.shape[1]), lambda i: (i, 0)), _resident(w_out.shape)] + ffn_specs
        args = (x2, attn_out, w_out) + ffn_args
    return pl.pallas_call(
        body,
        grid=(T // tm,),
        in_specs=specs,
        out_specs=row_tile,
        out_shape=jax.ShapeDtypeStruct((T, D), F32),
        compiler_params=_cparams(("parallel",)),
        name="ffn" if attn_out is None else "proj_ffn",
    )(*args)


def _pool_kernel(x_ref, halo_ref, g_ref, w_ref, sc_ref, o_ref, hbuf, *, tm):
    i = pl.program_id(1)
    gain = g_ref[...]
    x = x_ref[0]
    hn = _rms_rows(x, gain)
    hh = _rms_rows(halo_ref[0], gain)
    hbuf[0:POOL_HALO] = jnp.where(i > 0, hh, 0.0)
    hbuf[POOL_HALO:POOL_HALO + tm] = hn
    t = i * tm + lax.broadcasted_iota(jnp.int32, (tm, 1), 0)
    C = w_ref.shape[1]
    for g, w_len in enumerate(POOL_WINDOWS):
        cols = slice(g * C, (g + 1) * C)
        u = hn[:, cols]
        wsum = u
        for j in range(1, w_len):
            wsum = wsum + hbuf[POOL_HALO - j:POOL_HALO - j + tm, cols]
        count = jnp.minimum(t + 1, w_len).astype(F32)
        pooled = (wsum / count - u).astype(BF16)
        y = jnp.dot(pooled, w_ref[g], preferred_element_type=F32)
        o_ref[0, :, cols] = x[:, cols] + y * sc_ref[:, cols]


def _pool(x3, gain, w, scale, tm):
    B, S, D = x3.shape
    G, C, _ = w.shape
    hb = tm // POOL_HALO
    return pl.pallas_call(
        functools.partial(_pool_kernel, tm=tm),
        grid=(B, S // tm),
        in_specs=[
            pl.BlockSpec((1, tm, D), lambda b, i: (b, i, 0)),
            pl.BlockSpec((1, POOL_HALO, D), lambda b, i: (b, jnp.maximum(i * hb - 1, 0), 0)),
            _resident((1, D)),
            _resident((G, C, C)),
            _resident((1, D)),
        ],
        out_specs=pl.BlockSpec((1, tm, D), lambda b, i: (b, i, 0)),
        out_shape=jax.ShapeDtypeStruct((B, S, D), F32),
        scratch_shapes=[pltpu.VMEM((POOL_HALO + tm, D), F32)],
        compiler_params=_cparams(("parallel", "parallel")),
        name="pool",
    )(x3, x3, gain.reshape(1, D), w, scale.reshape(1, D))


def _rms_cols(xT, gain_col, n):
    ms = jnp.sum(xT * xT, axis=0, keepdims=True) * (1.0 / n)
    return xT * lax.rsqrt(ms + EPS) * gain_col


def _mla_pre_kernel(bound_ref, x_ref, pos_ref, invf_ref, gmix_ref, winT_ref, gq_ref, wqT_ref, gkv_ref, wkvT_ref,
                    gqh_ref, gkh_ref, qT_ref, k_ref, vT_ref, *, n_heads, q_lora, kv_lora, q_scale):
    tm = x_ref.shape[1]
    hn = _rms_rows(x_ref[0], gmix_ref[...]).astype(BF16)
    latT = lax.dot_general(winT_ref[...], hn, (((1,), (1,)), ((), ())), preferred_element_type=F32)
    cqT = latT[:q_lora]
    ckvT = latT[q_lora:q_lora + kv_lora]
    kpeT = latT[q_lora + kv_lora:]
    qT = jnp.dot(wqT_ref[...], _rms_cols(cqT, gq_ref[...], q_lora).astype(BF16),
                 preferred_element_type=F32)
    kvT = jnp.dot(wkvT_ref[...], _rms_cols(ckvT, gkv_ref[...], kv_lora).astype(BF16),
                  preferred_element_type=F32)

    ang = pos_ref[0].astype(F32) * invf_ref[...]
    cos, sin = jnp.cos(ang), jnp.sin(ang)
    half = QK_ROPE // 2
    gqh = gqh_ref[...]
    gkh = gkh_ref[...]
    kpe_ss = jnp.sum(kpeT * kpeT, axis=0, keepdims=True)
    first = lax.broadcasted_iota(jnp.int32, (QK_PAD - QK_HEAD, tm), 0) == 0
    q_pad = jnp.where(first, 1.0, 0.0)
    k_pad = jnp.where(first, -bound_ref[0], 0.0)

    def rope(x1, x2):
        return x1 * cos - x2 * sin, x2 * cos + x1 * sin

    for h in range(n_heads):
        qh = qT[h * QK_HEAD:(h + 1) * QK_HEAD]
        ms = jnp.sum(qh * qh, axis=0, keepdims=True) * (1.0 / QK_HEAD)
        qn = qh * (lax.rsqrt(ms + EPS) * q_scale) * gqh
        r1, r2 = rope(qn[QK_NOPE:QK_NOPE + half], qn[QK_NOPE + half:])
        qT_ref[0, h] = jnp.concatenate([qn[:QK_NOPE], r1, r2, q_pad], axis=0).astype(BF16)

        kn = kvT[h * (QK_NOPE + V_HEAD):h * (QK_NOPE + V_HEAD) + QK_NOPE]
        v = kvT[h * (QK_NOPE + V_HEAD) + QK_NOPE:(h + 1) * (QK_NOPE + V_HEAD)]
        ms = (jnp.sum(kn * kn, axis=0, keepdims=True) + kpe_ss) * (1.0 / QK_HEAD)
        rs = lax.rsqrt(ms + EPS)
        r1, r2 = rope(kpeT[:half] * rs * gkh[QK_NOPE:QK_NOPE + half],
                      kpeT[half:] * rs * gkh[QK_NOPE + half:])
        k_ref[0, h, 0, :, :QK_NOPE] = (kn * rs * gkh[:QK_NOPE]).T.astype(BF16)
        k_ref[0, h, 0, :, QK_NOPE:] = jnp.concatenate([r1, r2, k_pad], axis=0).T.astype(BF16)
        vT_ref[0, h, 0] = v.astype(BF16)


def _mla_pre(bound, x3, pos3, invf, gmix, winT, gq, wqT, gkv, wkvT, gqh, gkh, n_heads, tm):
    B, S, D = x3.shape
    q_lora, kv_lora = wqT.shape[1], wkvT.shape[1]
    n_t = S // tm
    kern = functools.partial(_mla_pre_kernel, n_heads=n_heads, q_lora=q_lora, kv_lora=kv_lora,
                             q_scale=Q_SCALE)
    return pl.pallas_call(
        kern,
        grid=(B, n_t),
        in_specs=[
            pl.BlockSpec(memory_space=pltpu.SMEM),
            pl.BlockSpec((1, tm, D), lambda b, i: (b, i, 0)),
            pl.BlockSpec((1, 1, tm), lambda b, i: (b, 0, i)),
            _resident(invf.shape),
            _resident((1, D)),
            _resident(winT.shape),
            _resident((q_lora, 1)),
            _resident(wqT.shape),
            _resident((kv_lora, 1)),
            _resident(wkvT.shape),
            _resident((QK_HEAD, 1)),
            _resident((QK_HEAD, 1)),
        ],
        out_specs=[
            pl.BlockSpec((1, n_heads, QK_PAD, tm), lambda b, i: (b, 0, 0, i)),
            pl.BlockSpec((1, n_heads, 1, tm, QK_PAD), lambda b, i: (b, 0, i, 0, 0)),
            pl.BlockSpec((1, n_heads, 1, V_HEAD, tm), lambda b, i: (b, 0, i, 0, 0)),
        ],
        out_shape=[
            jax.ShapeDtypeStruct((B, n_heads, QK_PAD, S), BF16),
            jax.ShapeDtypeStruct((B, n_heads, n_t, tm, QK_PAD), BF16),
            jax.ShapeDtypeStruct((B, n_heads, n_t, V_HEAD, tm), BF16),
        ],
        compiler_params=_cparams(("parallel", "parallel")),
        name="mla_pre",
    )(bound, x3, pos3, invf, gmix.reshape(1, D), winT, gq.reshape(-1, 1), wqT, gkv.reshape(-1, 1), wkvT,
      gqh.reshape(-1, 1), gkh.reshape(-1, 1))


def _attn_kernel(safe_ref, qT_ref, k_ref, vT_ref, o_ref, m_ref, l_ref, acc_ref, p0_ref, p1_ref):
    i = pl.program_id(2)
    sub = l_ref.shape[0]
    tk, tq = p0_ref.shape
    l_ref[...] = jnp.zeros(l_ref.shape, F32)
    acc_ref[...] = jnp.zeros(acc_ref.shape, F32)

    def scores(j, diag=None):
        sT = jnp.dot(k_ref[0, 0, j], qT_ref[0, 0], preferred_element_type=F32)
        if diag is None:
            return sT
        key = lax.broadcasted_iota(jnp.int32, sT.shape, 0) + diag * tk
        qry = lax.broadcasted_iota(jnp.int32, sT.shape, 1)
        return jnp.where(key <= qry, sT, -jnp.inf)

    def partial_sum(p):
        return jnp.sum(p.reshape(p.shape[0] // sub, sub, p.shape[1]), axis=0)

    def probs(j, p_ref, diag=None):
        p = jnp.exp2(scores(j, diag))
        l_ref[...] += partial_sum(p)
        p_ref[...] = p.astype(BF16)

    def probs_last_diag(j, p_ref):
        sT = jnp.dot(k_ref[0, 0, j], qT_ref[0, 0, :, tk:], preferred_element_type=F32)
        key = lax.broadcasted_iota(jnp.int32, sT.shape, 0)
        qry = lax.broadcasted_iota(jnp.int32, sT.shape, 1)
        p = jnp.exp2(jnp.where(key <= qry, sT, -jnp.inf))
        l_ref[:, tk:] += partial_sum(p)
        p_ref[:, :tk] = jnp.zeros((tk, tk), BF16)
        p_ref[:, tk:] = p.astype(BF16)

    def pv(j, p_ref):
        acc_ref[...] += jnp.dot(vT_ref[0, 0, j], p_ref[...], preferred_element_type=F32)

    def online(j, diag=None):
        sT = scores(j, diag)
        m_old = m_ref[...]
        m_new = jnp.maximum(m_old, jnp.max(sT, axis=0, keepdims=True))
        alpha = jnp.exp2(m_old - m_new)
        p = jnp.exp2(sT - m_new)
        l_ref[...] = alpha * l_ref[...] + partial_sum(p)
        acc_ref[...] = alpha * acc_ref[...] + jnp.dot(vT_ref[0, 0, j], p.astype(BF16),
                                                      preferred_element_type=F32)
        m_ref[...] = m_new

    @pl.when(safe_ref[0] != 0)
    def _():
        probs(2 * i, p0_ref, diag=0)
        probs_last_diag(2 * i + 1, p1_ref)
        pv(2 * i, p0_ref)

        def pair(t):
            probs(2 * t, p0_ref)
            pv(jnp.where(t == 0, 2 * i + 1, 2 * t - 1), p1_ref)
            probs(2 * t + 1, p1_ref)
            pv(2 * t, p0_ref)

        def two_pairs(t, carry):
            pair(2 * t)
            pair(2 * t + 1)
            return carry

        lax.fori_loop(0, i // 2, two_pairs, 0)

        @pl.when(i % 2 == 1)
        def _():
            pair(i - 1)

        pv(jnp.where(i == 0, 1, 2 * i - 1), p1_ref)

    @pl.when(safe_ref[0] == 0)
    def _():
        m_ref[...] = jnp.full(m_ref.shape, -jnp.inf, F32)

        def body(j, carry):
            online(j)
            return carry

        lax.fori_loop(0, 2 * i, body, 0)
        online(2 * i, diag=0)
        online(2 * i + 1, diag=1)

    l = jnp.sum(l_ref[...], axis=0, keepdims=True)
    o_ref[0] = (acc_ref[...] / l).T.astype(o_ref.dtype)


def _attn(safe, qT, k, vT):
    B, H, _, S = qT.shape
    n_k, tk = k.shape[2], k.shape[3]
    tq = 2 * tk
    assert S % tq == 0, (S, tq)
    return pl.pallas_call(
        _attn_kernel,
        grid=(B, H, S // tq),
        in_specs=[
            pl.BlockSpec(memory_space=pltpu.SMEM),
            pl.BlockSpec((1, 1, QK_PAD, tq), lambda b, h, i: (b, h, 0, i)),
            pl.BlockSpec((1, 1, n_k, tk, QK_PAD), lambda b, h, i: (b, h, 0, 0, 0)),
            pl.BlockSpec((1, 1, n_k, V_HEAD, tk), lambda b, h, i: (b, h, 0, 0, 0)),
        ],
        out_specs=pl.BlockSpec((1, tq, V_HEAD), lambda b, h, i: (b, i, h)),
        out_shape=jax.ShapeDtypeStruct((B, S, H * V_HEAD), BF16),
        scratch_shapes=[pltpu.VMEM((1, tq), F32), pltpu.VMEM((SUBLANES, tq), F32),
                        pltpu.VMEM((V_HEAD, tq), F32), pltpu.VMEM((tk, tq), BF16), pltpu.VMEM((tk, tq), BF16)],
        compiler_params=_cparams(("parallel", "parallel", "arbitrary")),
        name="attn",
    )(safe, qT, k, vT)


def _tile(n, want):
    t = min(n, want)
    assert n % t == 0, (n, t)
    return t


def kernel(x, positions, ffn1_norm, ffn1_w_gate, ffn1_w_up, ffn1_w_down, mix_norm, pool_w, pool_scale,
           mla_w_in, mla_q_norm, mla_w_q_up, mla_kv_norm, mla_w_kv_up, mla_q_head_norm, mla_k_head_norm,
           mla_w_out, ffn2_norm, ffn2_w_gate, ffn2_w_up, ffn2_w_down):
    B, S, D = x.shape
    depth = ffn1_norm.shape[0]
    n_heads = mla_w_out.shape[1] // V_HEAD
    T = B * S
    tm_ffn = _tile(T, 512)
    tm_pool = _tile(S, 512)
    tk = _tile(S // 2, 512)

    invf = (1.0 / (ROPE_THETA ** (jnp.arange(0, QK_ROPE, 2, dtype=F32) / QK_ROPE))).reshape(-1, 1)
    pos3 = positions.reshape(B, 1, S)

    for i in range(depth):
        x = _ffn(x.reshape(T, D), ffn1_norm[i], ffn1_w_gate[i].astype(BF16), ffn1_w_up[i].astype(BF16),
                 ffn1_w_down[i].astype(BF16), tm_ffn).reshape(B, S, D)
        j = i // 2
        proj = {}
        if i % 2 == 0:
            x = _pool(x, mix_norm[i], pool_w[j].astype(BF16), pool_scale[j], tm_pool)
        else:
            bound = (Q_SCALE * QK_HEAD * jnp.max(jnp.abs(mla_q_head_norm[j]))
                     * jnp.max(jnp.abs(mla_k_head_norm[j]))).reshape(1)
            safe = (2.0 * bound <= F32_SAFE_EXP2).astype(jnp.int32)
            qT, k, vT = _mla_pre(bound, x, pos3, invf, mix_norm[i], mla_w_in[j].T.astype(BF16), mla_q_norm[j],
                                 mla_w_q_up[j].T.astype(BF16), mla_kv_norm[j], mla_w_kv_up[j].T.astype(BF16),
                                 mla_q_head_norm[j], mla_k_head_norm[j], n_heads, tk)
            o = _attn(safe, qT, k, vT)
            proj = dict(attn_out=o.reshape(T, n_heads * V_HEAD), w_out=mla_w_out[j].astype(BF16))
        x = _ffn(x.reshape(T, D), ffn2_norm[i], ffn2_w_gate[i].astype(BF16), ffn2_w_up[i].astype(BF16),
                 ffn2_w_down[i].astype(BF16), tm_ffn, **proj).reshape(B, S, D)
    return x
```

```python
import functools
import math

import jax
import jax.numpy as jnp
from jax import lax
from jax.experimental import pallas as pl
from jax.experimental.pallas import tpu as pltpu

F32 = jnp.float32
BF16 = jnp.bfloat16

EPS = 1e-6
FFN_HALF = 0.5
FFN_CHUNK = 1024
POOL_WINDOWS = (2, 4, 8, 16)
POOL_HALO = 16
QK_NOPE = 128
QK_ROPE = 64
QK_HEAD = QK_NOPE + QK_ROPE
QK_PAD = 256
V_HEAD = 128
ROPE_THETA = 10000.0
LOG2E = math.log2(math.e)
Q_SCALE = QK_HEAD ** -0.5 * LOG2E
SUBLANES = 8
F32_SAFE_EXP2 = 120.0

VMEM_LIMIT = 56 * 1024 * 1024


def _cparams(sem):
    return pltpu.CompilerParams(dimension_semantics=sem, vmem_limit_bytes=VMEM_LIMIT)


def _resident(shape):
    nd = len(shape)
    return pl.BlockSpec(shape, lambda *_: (0,) * nd, pipeline_mode=pl.Buffered(1))


def _rms_rows(x, gain):
    ms = jnp.mean(x * x, axis=-1, keepdims=True)
    return x * lax.rsqrt(ms + EPS) * gain


def _swiglu_residual(x, g_ref, wg_ref, wu_ref, wd_ref):
    hn = _rms_rows(x, g_ref[...]).astype(BF16)
    d_ff = wg_ref.shape[1]
    y = None
    for c0 in range(0, d_ff, FFN_CHUNK):
        c1 = min(c0 + FFN_CHUNK, d_ff)
        g = jnp.dot(hn, wg_ref[:, c0:c1], preferred_element_type=F32)
        u = jnp.dot(hn, wu_ref[:, c0:c1], preferred_element_type=F32)
        a = (g / (1.0 + jnp.exp(-g)) * u).astype(BF16)
        yc = jnp.dot(a, wd_ref[c0:c1, :], preferred_element_type=F32)
        y = yc if y is None else y + yc
    return x + FFN_HALF * y


def _ffn_kernel(x_ref, g_ref, wg_ref, wu_ref, wd_ref, o_ref):
    o_ref[...] = _swiglu_residual(x_ref[...], g_ref, wg_ref, wu_ref, wd_ref)


def _proj_ffn_kernel(x_ref, a_ref, wo_ref, g_ref, wg_ref, wu_ref, wd_ref, o_ref):
    x = x_ref[...] + jnp.dot(a_ref[...], wo_ref[...], preferred_element_type=F32)
    o_ref[...] = _swiglu_residual(x, g_ref, wg_ref, wu_ref, wd_ref)


def _ffn(x2, gain, wg, wu, wd, tm, attn_out=None, w_out=None):
    T, D = x2.shape
    F = wg.shape[1]
    row_tile = pl.BlockSpec((tm, D), lambda i: (i, 0))
    ffn_specs = [_resident((1, D)), _resident((D, F)), _resident((D, F)), _resident((F, D))]
    ffn_args = (gain.reshape(1, D), wg, wu, wd)
    if attn_out is None:
        body, specs, args = _ffn_kernel, [row_tile] + ffn_specs, (x2,) + ffn_args
    else:
        body = _proj_ffn_kernel
        specs = [row_tile, pl.BlockSpec((tm, attn_out.shape[1]), lambda i: (i, 0)), _resident(w_out.shape)] + ffn_specs
        args = (x2, attn_out, w_out) + ffn_args
    return pl.pallas_call(
        body,
        grid=(T // tm,),
        in_specs=specs,
        out_specs=row_tile,
        out_shape=jax.ShapeDtypeStruct((T, D), F32),
        compiler_params=_cparams(("parallel",)),
        name="ffn" if attn_out is None else "proj_ffn",
    )(*args)


def _pool_kernel(x_ref, halo_ref, g_ref, w_ref, sc_ref, o_ref, hbuf, *, tm):
    i = pl.program_id(1)
    gain = g_ref[...]
    x = x_ref[0]
    hn = _rms_rows(x, gain)
    hh = _rms_rows(halo_ref[0], gain)
    hbuf[0:POOL_HALO] = jnp.where(i > 0, hh, 0.0)
    hbuf[POOL_HALO:POOL_HALO + tm] = hn
    t = i * tm + lax.broadcasted_iota(jnp.int32, (tm, 1), 0)
    C = w_ref.shape[1]
    for g, w_len in enumerate(POOL_WINDOWS):
        cols = slice(g * C, (g + 1) * C)
        u = hn[:, cols]
        wsum = u
        for j in range(1, w_len):
            wsum = wsum + hbuf[POOL_HALO - j:POOL_HALO - j + tm, cols]
        count = jnp.minimum(t + 1, w_len).astype(F32)
        pooled = (wsum / count - u).astype(BF16)
        y = jnp.dot(pooled, w_ref[g], preferred_element_type=F32)
        o_ref[0, :, cols] = x[:, cols] + y * sc_ref[:, cols]


def _pool(x3, gain, w, scale, tm):
    B, S, D = x3.shape
    G, C, _ = w.shape
    hb = tm // POOL_HALO
    return pl.pallas_call(
        functools.partial(_pool_kernel, tm=tm),
        grid=(B, S // tm),
        in_specs=[
            pl.BlockSpec((1, tm, D), lambda b, i: (b, i, 0)),
            pl.BlockSpec((1, POOL_HALO, D), lambda b, i: (b, jnp.maximum(i * hb - 1, 0), 0)),
            _resident((1, D)),
            _resident((G, C, C)),
            _resident((1, D)),
        ],
        out_specs=pl.BlockSpec((1, tm, D), lambda b, i: (b, i, 0)),
        out_shape=jax.ShapeDtypeStruct((B, S, D), F32),
        scratch_shapes=[pltpu.VMEM((POOL_HALO + tm, D), F32)],
        compiler_params=_cparams(("parallel", "parallel")),
        name="pool",
    )(x3, x3, gain.reshape(1, D), w, scale.reshape(1, D))


def _rms_cols(xT, gain_col, n):
    ms = jnp.sum(xT * xT, axis=0, keepdims=True) * (1.0 / n)
    return xT * lax.rsqrt(ms + EPS) * gain_col


def _mla_pre_kernel(bound_ref, x_ref, pos_ref, invf_ref, gmix_ref, winT_ref, gq_ref, wqT_ref, gkv_ref, wkvT_ref,
                    gqh_ref, gkh_ref, qT_ref, k_ref, vT_ref, *, n_heads, q_lora, kv_lora, q_scale):
    tm = x_ref.shape[1]
    hn = _rms_rows(x_ref[0], gmix_ref[...]).astype(BF16)
    latT = lax.dot_general(winT_ref[...], hn, (((1,), (1,)), ((), ())), preferred_element_type=F32)
    cqT = latT[:q_lora]
    ckvT = latT[q_lora:q_lora + kv_lora]
    kpeT = latT[q_lora + kv_lora:]
    qT = jnp.dot(wqT_ref[...], _rms_cols(cqT, gq_ref[...], q_lora).astype(BF16),
                 preferred_element_type=F32)
    kvT = jnp.dot(wkvT_ref[...], _rms_cols(ckvT, gkv_ref[...], kv_lora).astype(BF16),
                  preferred_element_type=F32)

    ang = pos_ref[0].astype(F32) * invf_ref[...]
    cos, sin = jnp.cos(ang), jnp.sin(ang)
    half = QK_ROPE // 2
    gqh = gqh_ref[...]
    gkh = gkh_ref[...]
    kpe_ss = jnp.sum(kpeT * kpeT, axis=0, keepdims=True)
    first = lax.broadcasted_iota(jnp.int32, (QK_PAD - QK_HEAD, tm), 0) == 0
    q_pad = jnp.where(first, 1.0, 0.0)
    k_pad = jnp.where(first, -bound_ref[0], 0.0)

    def rope(x1, x2):
        return x1 * cos - x2 * sin, x2 * cos + x1 * sin

    for h in range(n_heads):
        qh = qT[h * QK_HEAD:(h + 1) * QK_HEAD]
        ms = jnp.sum(qh * qh, axis=0, keepdims=True) * (1.0 / QK_HEAD)
        qn = qh * (lax.rsqrt(ms + EPS) * q_scale) * gqh
        r1, r2 = rope(qn[QK_NOPE:QK_NOPE + half], qn[QK_NOPE + half:])
        qT_ref[0, h] = jnp.concatenate([qn[:QK_NOPE], r1, r2, q_pad], axis=0).astype(BF16)

        kn = kvT[h * (QK_NOPE + V_HEAD):h * (QK_NOPE + V_HEAD) + QK_NOPE]
        v = kvT[h * (QK_NOPE + V_HEAD) + QK_NOPE:(h + 1) * (QK_NOPE + V_HEAD)]
        ms = (jnp.sum(kn * kn, axis=0, keepdims=True) + kpe_ss) * (1.0 / QK_HEAD)
        rs = lax.rsqrt(ms + EPS)
        r1, r2 = rope(kpeT[:half] * rs * gkh[QK_NOPE:QK_NOPE + half],
                      kpeT[half:] * rs * gkh[QK_NOPE + half:])
        k_ref[0, h, 0, :, :QK_NOPE] = (kn * rs * gkh[:QK_NOPE]).T.astype(BF16)
        k_ref[0, h, 0, :, QK_NOPE:] = jnp.concatenate([r1, r2, k_pad], axis=0).T.astype(BF16)
        vT_ref[0, h, 0] = v.astype(BF16)


def _mla_pre(bound, x3, pos3, invf, gmix, winT, gq, wqT, gkv, wkvT, gqh, gkh, n_heads, tm):
    B, S, D = x3.shape
    q_lora, kv_lora = wqT.shape[1], wkvT.shape[1]
    n_t = S // tm
    kern = functools.partial(_mla_pre_kernel, n_heads=n_heads, q_lora=q_lora, kv_lora=kv_lora,
                             q_scale=Q_SCALE)
    return pl.pallas_call(
        kern,
        grid=(B, n_t),
        in_specs=[
            pl.BlockSpec(memory_space=pltpu.SMEM),
            pl.BlockSpec((1, tm, D), lambda b, i: (b, i, 0)),
            pl.BlockSpec((1, 1, tm), lambda b, i: (b, 0, i)),
            _resident(invf.shape),
            _resident((1, D)),
            _resident(winT.shape),
            _resident((q_lora, 1)),
            _resident(wqT.shape),
            _resident((kv_lora, 1)),
            _resident(wkvT.shape),
            _resident((QK_HEAD, 1)),
            _resident((QK_HEAD, 1)),
        ],
        out_specs=[
            pl.BlockSpec((1, n_heads, QK_PAD, tm), lambda b, i: (b, 0, 0, i)),
            pl.BlockSpec((1, n_heads, 1, tm, QK_PAD), lambda b, i: (b, 0, i, 0, 0)),
            pl.BlockSpec((1, n_heads, 1, V_HEAD, tm), lambda b, i: (b, 0, i, 0, 0)),
        ],
        out_shape=[
            jax.ShapeDtypeStruct((B, n_heads, QK_PAD, S), BF16),
            jax.ShapeDtypeStruct((B, n_heads, n_t, tm, QK_PAD), BF16),
            jax.ShapeDtypeStruct((B, n_heads, n_t, V_HEAD, tm), BF16),
        ],
        compiler_params=_cparams(("parallel", "parallel")),
        name="mla_pre",
    )(bound, x3, pos3, invf, gmix.reshape(1, D), winT, gq.reshape(-1, 1), wqT, gkv.reshape(-1, 1), wkvT,
      gqh.reshape(-1, 1), gkh.reshape(-1, 1))


def _attn_kernel(safe_ref, qT_ref, k_ref, vT_ref, o_ref, m_ref, l_ref, acc_ref, p0_ref, p1_ref):
    i = pl.program_id(2)
    sub = l_ref.shape[0]
    tk, tq = p0_ref.shape
    l_ref[...] = jnp.zeros(l_ref.shape, F32)
    acc_ref[...] = jnp.zeros(acc_ref.shape, F32)

    def scores(j, diag=None):
        sT = jnp.dot(k_ref[0, 0, j], qT_ref[0, 0], preferred_element_type=F32)
        if diag is None:
            return sT
        key = lax.broadcasted_iota(jnp.int32, sT.shape, 0) + diag * tk
        qry = lax.broadcasted_iota(jnp.int32, sT.shape, 1)
        return jnp.where(key <= qry, sT, -jnp.inf)

    def partial_sum(p):
        return jnp.sum(p.reshape(p.shape[0] // sub, sub, p.shape[1]), axis=0)

    def probs(j, p_ref, diag=None):
        p = jnp.exp2(scores(j, diag))
        l_ref[...] += partial_sum(p)
        p_ref[...] = p.astype(BF16)

    def probs_last_diag(j, p_ref):
        sT = jnp.dot(k_ref[0, 0, j], qT_ref[0, 0, :, tk:], preferred_element_type=F32)
        key = lax.broadcasted_iota(jnp.int32, sT.shape, 0)
        qry = lax.broadcasted_iota(jnp.int32, sT.shape, 1)
        p = jnp.exp2(jnp.where(key <= qry, sT, -jnp.inf))
        l_ref[:, tk:] += partial_sum(p)
        p_ref[:, :tk] = jnp.zeros((tk, tk), BF16)
        p_ref[:, tk:] = p.astype(BF16)

    def pv(j, p_ref):
        acc_ref[...] += jnp.dot(vT_ref[0, 0, j], p_ref[...], preferred_element_type=F32)

    def online(j, diag=None):
        sT = scores(j, diag)
        m_old = m_ref[...]
        m_new = jnp.maximum(m_old, jnp.max(sT, axis=0, keepdims=True))
        alpha = jnp.exp2(m_old - m_new)
        p = jnp.exp2(sT - m_new)
        l_ref[...] = alpha * l_ref[...] + partial_sum(p)
        acc_ref[...] = alpha * acc_ref[...] + jnp.dot(vT_ref[0, 0, j], p.astype(BF16),
                                                      preferred_element_type=F32)
        m_ref[...] = m_new

    @pl.when(safe_ref[0] != 0)
    def _():
        probs(2 * i, p0_ref, diag=0)
        probs_last_diag(2 * i + 1, p1_ref)
        pv(2 * i, p0_ref)

        def pair(t):
            probs(2 * t, p0_ref)
            pv(jnp.where(t == 0, 2 * i + 1, 2 * t - 1), p1_ref)
            probs(2 * t + 1, p1_ref)
            pv(2 * t, p0_ref)

        def two_pairs(t, carry):
            pair(2 * t)
            pair(2 * t + 1)
            return carry

        lax.fori_loop(0, i // 2, two_pairs, 0)

        @pl.when(i % 2 == 1)
        def _():
            pair(i - 1)

        pv(jnp.where(i == 0, 1, 2 * i - 1), p1_ref)

    @pl.when(safe_ref[0] == 0)
    def _():
        m_ref[...] = jnp.full(m_ref.shape, -jnp.inf, F32)

        def body(j, carry):
            online(j)
            return carry

        lax.fori_loop(0, 2 * i, body, 0)
        online(2 * i, diag=0)
        online(2 * i + 1, diag=1)

    l = jnp.sum(l_ref[...], axis=0, keepdims=True)
    o_ref[0] = (acc_ref[...] / l).T.astype(o_ref.dtype)


def _attn(safe, qT, k, vT):
    B, H, _, S = qT.shape
    n_k, tk = k.shape[2], k.shape[3]
    tq = 2 * tk
    assert S % tq == 0, (S, tq)
    return pl.pallas_call(
        _attn_kernel,
        grid=(B, H, S // tq),
        in_specs=[
            pl.BlockSpec(memory_space=pltpu.SMEM),
            pl.BlockSpec((1, 1, QK_PAD, tq), lambda b, h, i: (b, h, 0, i)),
            pl.BlockSpec((1, 1, n_k, tk, QK_PAD), lambda b, h, i: (b, h, 0, 0, 0)),
            pl.BlockSpec((1, 1, n_k, V_HEAD, tk), lambda b, h, i: (b, h, 0, 0, 0)),
        ],
        out_specs=pl.BlockSpec((1, tq, V_HEAD), lambda b, h, i: (b, i, h)),
        out_shape=jax.ShapeDtypeStruct((B, S, H * V_HEAD), BF16),
        scratch_shapes=[pltpu.VMEM((1, tq), F32), pltpu.VMEM((SUBLANES, tq), F32),
                        pltpu.VMEM((V_HEAD, tq), F32), pltpu.VMEM((tk, tq), BF16), pltpu.VMEM((tk, tq), BF16)],
        compiler_params=_cparams(("parallel", "parallel", "arbitrary")),
        name="attn",
    )(safe, qT, k, vT)


def _tile(n, want):
    t = min(n, want)
    assert n % t == 0, (n, t)
    return t


def kernel(x, positions, ffn1_norm, ffn1_w_gate, ffn1_w_up, ffn1_w_down, mix_norm, pool_w, pool_scale,
           mla_w_in, mla_q_norm, mla_w_q_up, mla_kv_norm, mla_w_kv_up, mla_q_head_norm, mla_k_head_norm,
           mla_w_out, ffn2_norm, ffn2_w_gate, ffn2_w_up, ffn2_w_down):
    B, S, D = x.shape
    depth = ffn1_norm.shape[0]
    n_heads = mla_w_out.shape[1] // V_HEAD
    T = B * S
    tm_ffn = _tile(T, 1024)
    tm_pool = _tile(S, 512)
    tk = _tile(S // 2, 512)

    invf = (1.0 / (ROPE_THETA ** (jnp.arange(0, QK_ROPE, 2, dtype=F32) / QK_ROPE))).reshape(-1, 1)
    pos3 = positions.reshape(B, 1, S)

    for i in range(depth):
        x = _ffn(x.reshape(T, D), ffn1_norm[i], ffn1_w_gate[i].astype(BF16), ffn1_w_up[i].astype(BF16),
                 ffn1_w_down[i].astype(BF16), tm_ffn).reshape(B, S, D)
        j = i // 2
        proj = {}
        if i % 2 == 0:
            x = _pool(x, mix_norm[i], pool_w[j].astype(BF16), pool_scale[j], tm_pool)
        else:
            bound = (Q_SCALE * QK_HEAD * jnp.max(jnp.abs(mla_q_head_norm[j]))
                     * jnp.max(jnp.abs(mla_k_head_norm[j]))).reshape(1)
            safe = (2.0 * bound <= F32_SAFE_EXP2).astype(jnp.int32)
            qT, k, vT = _mla_pre(bound, x, pos3, invf, mix_norm[i], mla_w_in[j].T.astype(BF16), mla_q_norm[j],
                                 mla_w_q_up[j].T.astype(BF16), mla_kv_norm[j], mla_w_kv_up[j].T.astype(BF16),
                                 mla_q_head_norm[j], mla_k_head_norm[j], n_heads, tk)
            o = _attn(safe, qT, k, vT)
            proj = dict(attn_out=o.reshape(T, n_heads * V_HEAD), w_out=mla_w_out[j].astype(BF16))
        x = _ffn(x.reshape(T, D), ffn2_norm[i], ffn2_w_gate[i].astype(BF16), ffn2_w_up[i].astype(BF16),
                 ffn2_w_down[i].astype(BF16), tm_ffn, **proj).reshape(B, S, D)
    return x
```

```python
import functools
import math

import jax
import jax.numpy as jnp
from jax import lax
from jax.experimental import pallas as pl
from jax.experimental.pallas import tpu as pltpu

F32 = jnp.float32
BF16 = jnp.bfloat16

EPS = 1e-6
FFN_HALF = 0.5
FFN_CHUNK = 1024
POOL_FFN_CHUNK = 256
POOL_WINDOWS = (2, 4, 8, 16)
POOL_HALO = 16
QK_NOPE = 128
QK_ROPE = 64
QK_HEAD = QK_NOPE + QK_ROPE
QK_PAD = 256
V_HEAD = 128
ROPE_THETA = 10000.0
LOG2E = math.log2(math.e)
Q_SCALE = QK_HEAD ** -0.5 * LOG2E
SUBLANES = 8
F32_SAFE_EXP2 = 120.0

VMEM_LIMIT = 56 * 1024 * 1024


def _cparams(sem):
    return pltpu.CompilerParams(dimension_semantics=sem, vmem_limit_bytes=VMEM_LIMIT)


def _resident(shape):
    nd = len(shape)
    return pl.BlockSpec(shape, lambda *_: (0,) * nd, pipeline_mode=pl.Buffered(1))


def _rms_rows(x, gain):
    ms = jnp.mean(x * x, axis=-1, keepdims=True)
    return x * lax.rsqrt(ms + EPS) * gain


def _swiglu_residual(x, g_ref, wg_ref, wu_ref, wd_ref, chunk=FFN_CHUNK, side_work=None):
    hn = _rms_rows(x, g_ref[...]).astype(BF16)
    d_ff = wg_ref.shape[1]
    y = None
    for c0 in range(0, d_ff, chunk):
        c1 = min(c0 + chunk, d_ff)
        g = jnp.dot(hn, wg_ref[:, c0:c1], preferred_element_type=F32)
        u = jnp.dot(hn, wu_ref[:, c0:c1], preferred_element_type=F32)
        a = (g / (1.0 + jnp.exp(-g)) * u).astype(BF16)
        yc = jnp.dot(a, wd_ref[c0:c1, :], preferred_element_type=F32)
        y = yc if y is None else y + yc
        if side_work is not None:
            next(side_work, None)
    if side_work is not None:
        for _ in side_work:
            pass
    return x + FFN_HALF * y


def _ffn_kernel(x_ref, g_ref, wg_ref, wu_ref, wd_ref, o_ref):
    o_ref[...] = _swiglu_residual(x_ref[...], g_ref, wg_ref, wu_ref, wd_ref)


def _proj_ffn_kernel(x_ref, a_ref, wo_ref, g_ref, wg_ref, wu_ref, wd_ref, o_ref):
    x = x_ref[...] + jnp.dot(a_ref[...], wo_ref[...], preferred_element_type=F32)
    o_ref[...] = _swiglu_residual(x, g_ref, wg_ref, wu_ref, wd_ref)


def _ffn(x2, gain, wg, wu, wd, tm, attn_out=None, w_out=None):
    T, D = x2.shape
    F = wg.shape[1]
    row_tile = pl.BlockSpec((tm, D), lambda i: (i, 0))
    ffn_specs = [_resident((1, D)), _resident((D, F)), _resident((D, F)), _resident((F, D))]
    ffn_args = (gain.reshape(1, D), wg, wu, wd)
    if attn_out is None:
        body, specs, args = _ffn_kernel, [row_tile] + ffn_specs, (x2,) + ffn_args
    else:
        body = _proj_ffn_kernel
        specs = [row_tile, pl.BlockSpec((tm, attn_out.shape[1]), lambda i: (i, 0)), _resident(w_out.shape)] + ffn_specs
        args = (x2, attn_out, w_out) + ffn_args
    return pl.pallas_call(
        body,
        grid=(T // tm,),
        in_specs=specs,
        out_specs=row_tile,
        out_shape=jax.ShapeDtypeStruct((T, D), F32),
        compiler_params=_cparams(("parallel",)),
        name="ffn" if attn_out is None else "proj_ffn",
    )(*args)


def _pool_mix(x, halo, first_of_seq, row0, g_ref, w_ref, sc_ref, hbuf, dst_ref):
    tm = x.shape[0]
    gain = g_ref[...]
    hn = _rms_rows(x, gain)
    hh = _rms_rows(halo, gain)
    hbuf[0:POOL_HALO] = jnp.where(first_of_seq, 0.0, hh)
    hbuf[POOL_HALO:POOL_HALO + tm] = hn
    yield
    C = w_ref.shape[1]
    rows = tm // 2
    for g, w_len in enumerate(POOL_WINDOWS):
        cols = slice(g * C, (g + 1) * C)
        for r0 in (0, rows):
            u = hbuf[POOL_HALO + r0:POOL_HALO + r0 + rows, cols]
            wsum = u
            for j in range(1, w_len):
                wsum = wsum + hbuf[POOL_HALO + r0 - j:POOL_HALO + r0 - j + rows, cols]
            t = row0 + r0 + lax.broadcasted_iota(jnp.int32, (rows, 1), 0)
            count = jnp.minimum(t + 1, w_len).astype(F32)
            pooled = (wsum / count - u).astype(BF16)
            y = jnp.dot(pooled, w_ref[g], preferred_element_type=F32)
            dst_ref[r0:r0 + rows, cols] = x[r0:r0 + rows, cols] + y * sc_ref[:, cols]
            yield


def _pool_ffn_kernel(x_ref, halo_ref, gmix_ref, pw_ref, psc_ref, g_ref, wg_ref, wu_ref, wd_ref, o_ref,
                     hbuf, xm0_ref, xm1_ref, *, n_tiles, tiles_per_seq):
    s = pl.program_id(0)
    tm = x_ref.shape[0]

    def mix_into(dst_ref):
        i = s % tiles_per_seq
        return _pool_mix(x_ref[...], halo_ref[...], i == 0, i * tm, gmix_ref, pw_ref, psc_ref, hbuf, dst_ref)

    def ffn_from(src_ref, side_work=None):
        o_ref[...] = _swiglu_residual(src_ref[...], g_ref, wg_ref, wu_ref, wd_ref, chunk=POOL_FFN_CHUNK,
                                      side_work=side_work)

    last_ref = xm0_ref if (n_tiles - 1) % 2 == 0 else xm1_ref
    middle = jnp.logical_and(s > 0, s < n_tiles)

    @pl.when(s == 0)
    def _():
        for _ in mix_into(xm0_ref):
            pass

    @pl.when(jnp.logical_and(middle, s % 2 == 1))
    def _():
        ffn_from(xm0_ref, mix_into(xm1_ref))

    @pl.when(jnp.logical_and(middle, s % 2 == 0))
    def _():
        ffn_from(xm1_ref, mix_into(xm0_ref))

    @pl.when(s == n_tiles)
    def _():
        ffn_from(last_ref)


def _pool_ffn(x2, seq_len, gmix, pool_w, pool_scale, gain, wg, wu, wd, tm):
    T, D = x2.shape
    F = wg.shape[1]
    G, C, _ = pool_w.shape
    n_tiles = T // tm
    hb = tm // POOL_HALO
    kern = functools.partial(_pool_ffn_kernel, n_tiles=n_tiles, tiles_per_seq=seq_len // tm)
    return pl.pallas_call(
        kern,
        grid=(n_tiles + 1,),
        in_specs=[
            pl.BlockSpec((tm, D), lambda s: (jnp.minimum(s, n_tiles - 1), 0)),
            pl.BlockSpec((POOL_HALO, D), lambda s: (jnp.maximum(jnp.minimum(s, n_tiles - 1) * hb - 1, 0), 0)),
            _resident((1, D)),
            _resident((G, C, C)),
            _resident((1, D)),
            _resident((1, D)),
            _resident((D, F)),
            _resident((D, F)),
            _resident((F, D)),
        ],
        out_specs=pl.BlockSpec((tm, D), lambda s: (jnp.maximum(s - 1, 0), 0)),
        out_shape=jax.ShapeDtypeStruct((T, D), F32),
        scratch_shapes=[pltpu.VMEM((POOL_HALO + tm, D), F32), pltpu.VMEM((tm, D), F32), pltpu.VMEM((tm, D), F32)],
        compiler_params=_cparams(("arbitrary",)),
        name="pool_ffn",
    )(x2, x2, gmix.reshape(1, D), pool_w, pool_scale.reshape(1, D), gain.reshape(1, D), wg, wu, wd)


def _rms_cols(xT, gain_col, n):
    ms = jnp.sum(xT * xT, axis=0, keepdims=True) * (1.0 / n)
    return xT * lax.rsqrt(ms + EPS) * gain_col


def _mla_pre_kernel(bound_ref, x_ref, pos_ref, invf_ref, gmix_ref, winT_ref, gq_ref, wqT_ref, gkv_ref, wkvT_ref,
                    gqh_ref, gkh_ref, qT_ref, k_ref, vT_ref, *, n_heads, q_lora, kv_lora, q_scale):
    tm = x_ref.shape[1]
    hn = _rms_rows(x_ref[0], gmix_ref[...]).astype(BF16)
    latT = lax.dot_general(winT_ref[...], hn, (((1,), (1,)), ((), ())), preferred_element_type=F32)
    cqT = latT[:q_lora]
    ckvT = latT[q_lora:q_lora + kv_lora]
    kpeT = latT[q_lora + kv_lora:]
    qT = jnp.dot(wqT_ref[...], _rms_cols(cqT, gq_ref[...], q_lora).astype(BF16),
                 preferred_element_type=F32)
    kvT = jnp.dot(wkvT_ref[...], _rms_cols(ckvT, gkv_ref[...], kv_lora).astype(BF16),
                  preferred_element_type=F32)

    ang = pos_ref[0].astype(F32) * invf_ref[...]
    cos, sin = jnp.cos(ang), jnp.sin(ang)
    half = QK_ROPE // 2
    gqh = gqh_ref[...]
    gkh = gkh_ref[...]
    kpe_ss = jnp.sum(kpeT * kpeT, axis=0, keepdims=True)
    first = lax.broadcasted_iota(jnp.int32, (QK_PAD - QK_HEAD, tm), 0) == 0
    q_pad = jnp.where(first, 1.0, 0.0)
    k_pad = jnp.where(first, -bound_ref[0], 0.0)

    def rope(x1, x2):
        return x1 * cos - x2 * sin, x2 * cos + x1 * sin

    for h in range(n_heads):
        qh = qT[h * QK_HEAD:(h + 1) * QK_HEAD]
        ms = jnp.sum(qh * qh, axis=0, keepdims=True) * (1.0 / QK_HEAD)
        qn = qh * (lax.rsqrt(ms + EPS) * q_scale) * gqh
        r1, r2 = rope(qn[QK_NOPE:QK_NOPE + half], qn[QK_NOPE + half:])
        qT_ref[0, h] = jnp.concatenate([qn[:QK_NOPE], r1, r2, q_pad], axis=0).astype(BF16)

        kn = kvT[h * (QK_NOPE + V_HEAD):h * (QK_NOPE + V_HEAD) + QK_NOPE]
        v = kvT[h * (QK_NOPE + V_HEAD) + QK_NOPE:(h + 1) * (QK_NOPE + V_HEAD)]
        ms = (jnp.sum(kn * kn, axis=0, keepdims=True) + kpe_ss) * (1.0 / QK_HEAD)
        rs = lax.rsqrt(ms + EPS)
        r1, r2 = rope(kpeT[:half] * rs * gkh[QK_NOPE:QK_NOPE + half],
                      kpeT[half:] * rs * gkh[QK_NOPE + half:])
        k_ref[0, h, 0, :, :QK_NOPE] = (kn * rs * gkh[:QK_NOPE]).T.astype(BF16)
        k_ref[0, h, 0, :, QK_NOPE:] = jnp.concatenate([r1, r2, k_pad], axis=0).T.astype(BF16)
        vT_ref[0, h, 0] = v.astype(BF16)


def _mla_pre(bound, x3, pos3, invf, gmix, winT, gq, wqT, gkv, wkvT, gqh, gkh, n_heads, tm):
    B, S, D = x3.shape
    q_lora, kv_lora = wqT.shape[1], wkvT.shape[1]
    n_t = S // tm
    kern = functools.partial(_mla_pre_kernel, n_heads=n_heads, q_lora=q_lora, kv_lora=kv_lora,
                             q_scale=Q_SCALE)
    return pl.pallas_call(
        kern,
        grid=(B, n_t),
        in_specs=[
            pl.BlockSpec(memory_space=pltpu.SMEM),
            pl.BlockSpec((1, tm, D), lambda b, i: (b, i, 0)),
            pl.BlockSpec((1, 1, tm), lambda b, i: (b, 0, i)),
            _resident(invf.shape),
            _resident((1, D)),
            _resident(winT.shape),
            _resident((q_lora, 1)),
            _resident(wqT.shape),
            _resident((kv_lora, 1)),
            _resident(wkvT.shape),
            _resident((QK_HEAD, 1)),
            _resident((QK_HEAD, 1)),
        ],
        out_specs=[
            pl.BlockSpec((1, n_heads, QK_PAD, tm), lambda b, i: (b, 0, 0, i)),
            pl.BlockSpec((1, n_heads, 1, tm, QK_PAD), lambda b, i: (b, 0, i, 0, 0)),
            pl.BlockSpec((1, n_heads, 1, V_HEAD, tm), lambda b, i: (b, 0, i, 0, 0)),
        ],
        out_shape=[
            jax.ShapeDtypeStruct((B, n_heads, QK_PAD, S), BF16),
            jax.ShapeDtypeStruct((B, n_heads, n_t, tm, QK_PAD), BF16),
            jax.ShapeDtypeStruct((B, n_heads, n_t, V_HEAD, tm), BF16),
        ],
        compiler_params=_cparams(("parallel", "parallel")),
        name="mla_pre",
    )(bound, x3, pos3, invf, gmix.reshape(1, D), winT, gq.reshape(-1, 1), wqT, gkv.reshape(-1, 1), wkvT,
      gqh.reshape(-1, 1), gkh.reshape(-1, 1))


def _attn_kernel(safe_ref, qT_ref, k_ref, vT_ref, o_ref, m_ref, l_ref, acc_ref, p0_ref, p1_ref):
    i = pl.program_id(2)
    sub = l_ref.shape[0]
    tk, tq = p0_ref.shape
    l_ref[...] = jnp.zeros(l_ref.shape, F32)
    acc_ref[...] = jnp.zeros(acc_ref.shape, F32)

    def scores(j, diag=None):
        sT = jnp.dot(k_ref[0, 0, j], qT_ref[0, 0], preferred_element_type=F32)
        if diag is None:
            return sT
        key = lax.broadcasted_iota(jnp.int32, sT.shape, 0) + diag * tk
        qry = lax.broadcasted_iota(jnp.int32, sT.shape, 1)
        return jnp.where(key <= qry, sT, -jnp.inf)

    def partial_sum(p):
        return jnp.sum(p.reshape(p.shape[0] // sub, sub, p.shape[1]), axis=0)

    def probs(j, p_ref, diag=None):
        p = jnp.exp2(scores(j, diag))
        l_ref[...] += partial_sum(p)
        p_ref[...] = p.astype(BF16)

    def probs_last_diag(j, p_ref):
        sT = jnp.dot(k_ref[0, 0, j], qT_ref[0, 0, :, tk:], preferred_element_type=F32)
        key = lax.broadcasted_iota(jnp.int32, sT.shape, 0)
        qry = lax.broadcasted_iota(jnp.int32, sT.shape, 1)
        p = jnp.exp2(jnp.where(key <= qry, sT, -jnp.inf))
        l_ref[:, tk:] += partial_sum(p)
        p_ref[:, :tk] = jnp.zeros((tk, tk), BF16)
        p_ref[:, tk:] = p.astype(BF16)

    def pv(j, p_ref):
        acc_ref[...] += jnp.dot(vT_ref[0, 0, j], p_ref[...], preferred_element_type=F32)

    def online(j, diag=None):
        sT = scores(j, diag)
        m_old = m_ref[...]
        m_new = jnp.maximum(m_old, jnp.max(sT, axis=0, keepdims=True))
        alpha = jnp.exp2(m_old - m_new)
        p = jnp.exp2(sT - m_new)
        l_ref[...] = alpha * l_ref[...] + partial_sum(p)
        acc_ref[...] = alpha * acc_ref[...] + jnp.dot(vT_ref[0, 0, j], p.astype(BF16),
                                                      preferred_element_type=F32)
        m_ref[...] = m_new

    @pl.when(safe_ref[0] != 0)
    def _():
        probs(2 * i, p0_ref, diag=0)
        probs_last_diag(2 * i + 1, p1_ref)
        pv(2 * i, p0_ref)

        def pair(t):
            probs(2 * t, p0_ref)
            pv(jnp.where(t == 0, 2 * i + 1, 2 * t - 1), p1_ref)
            probs(2 * t + 1, p1_ref)
            pv(2 * t, p0_ref)

        def two_pairs(t, carry):
            pair(2 * t)
            pair(2 * t + 1)
            return carry

        lax.fori_loop(0, i // 2, two_pairs, 0)

        @pl.when(i % 2 == 1)
        def _():
            pair(i - 1)

        pv(jnp.where(i == 0, 1, 2 * i - 1), p1_ref)

    @pl.when(safe_ref[0] == 0)
    def _():
        m_ref[...] = jnp.full(m_ref.shape, -jnp.inf, F32)

        def body(j, carry):
            online(j)
            return carry

        lax.fori_loop(0, 2 * i, body, 0)
        online(2 * i, diag=0)
        online(2 * i + 1, diag=1)

    l = jnp.sum(l_ref[...], axis=0, keepdims=True)
    o_ref[0] = (acc_ref[...] / l).T.astype(o_ref.dtype)


def _attn(safe, qT, k, vT):
    B, H, _, S = qT.shape
    n_k, tk = k.shape[2], k.shape[3]
    tq = 2 * tk
    assert S % tq == 0, (S, tq)
    return pl.pallas_call(
        _attn_kernel,
        grid=(B, H, S // tq),
        in_specs=[
            pl.BlockSpec(memory_space=pltpu.SMEM),
            pl.BlockSpec((1, 1, QK_PAD, tq), lambda b, h, i: (b, h, 0, i)),
            pl.BlockSpec((1, 1, n_k, tk, QK_PAD), lambda b, h, i: (b, h, 0, 0, 0)),
            pl.BlockSpec((1, 1, n_k, V_HEAD, tk), lambda b, h, i: (b, h, 0, 0, 0)),
        ],
        out_specs=pl.BlockSpec((1, tq, V_HEAD), lambda b, h, i: (b, i, h)),
        out_shape=jax.ShapeDtypeStruct((B, S, H * V_HEAD), BF16),
        scratch_shapes=[pltpu.VMEM((1, tq), F32), pltpu.VMEM((SUBLANES, tq), F32),
                        pltpu.VMEM((V_HEAD, tq), F32), pltpu.VMEM((tk, tq), BF16), pltpu.VMEM((tk, tq), BF16)],
        compiler_params=_cparams(("parallel", "parallel", "arbitrary")),
        name="attn",
    )(safe, qT, k, vT)


def _tile(n, want):
    t = min(n, want)
    assert n % t == 0, (n, t)
    return t


def kernel(x, positions, ffn1_norm, ffn1_w_gate, ffn1_w_up, ffn1_w_down, mix_norm, pool_w, pool_scale,
           mla_w_in, mla_q_norm, mla_w_q_up, mla_kv_norm, mla_w_kv_up, mla_q_head_norm, mla_k_head_norm,
           mla_w_out, ffn2_norm, ffn2_w_gate, ffn2_w_up, ffn2_w_down):
    B, S, D = x.shape
    depth = ffn1_norm.shape[0]
    n_heads = mla_w_out.shape[1] // V_HEAD
    T = B * S
    tm_ffn = _tile(T, 1024)
    tm_pool = _tile(S, 512)
    tk = _tile(S // 2, 512)

    invf = (1.0 / (ROPE_THETA ** (jnp.arange(0, QK_ROPE, 2, dtype=F32) / QK_ROPE))).reshape(-1, 1)
    pos3 = positions.reshape(B, 1, S)

    for i in range(depth):
        x = _ffn(x.reshape(T, D), ffn1_norm[i], ffn1_w_gate[i].astype(BF16), ffn1_w_up[i].astype(BF16),
                 ffn1_w_down[i].astype(BF16), tm_ffn).reshape(B, S, D)
        j = i // 2
        ffn2 = (ffn2_norm[i], ffn2_w_gate[i].astype(BF16), ffn2_w_up[i].astype(BF16), ffn2_w_down[i].astype(BF16))
        if i % 2 == 0:
            x = _pool_ffn(x.reshape(T, D), S, mix_norm[i], pool_w[j].astype(BF16), pool_scale[j], *ffn2,
                          tm_pool).reshape(B, S, D)
        else:
            bound = (Q_SCALE * QK_HEAD * jnp.max(jnp.abs(mla_q_head_norm[j]))
                     * jnp.max(jnp.abs(mla_k_head_norm[j]))).reshape(1)
            safe = (2.0 * bound <= F32_SAFE_EXP2).astype(jnp.int32)
            qT, k, vT = _mla_pre(bound, x, pos3, invf, mix_norm[i], mla_w_in[j].T.astype(BF16), mla_q_norm[j],
                                 mla_w_q_up[j].T.astype(BF16), mla_kv_norm[j], mla_w_kv_up[j].T.astype(BF16),
                                 mla_q_head_norm[j], mla_k_head_norm[j], n_heads, tk)
            o = _attn(safe, qT, k, vT)
            x = _ffn(x.reshape(T, D), *ffn2, tm_ffn, attn_out=o.reshape(T, n_heads * V_HEAD),
                     w_out=mla_w_out[j].astype(BF16)).reshape(B, S, D)
    return x
```

```python
import functools
import math

import jax
import jax.numpy as jnp
from jax import lax
from jax.experimental import pallas as pl
from jax.experimental.pallas import tpu as pltpu

F32 = jnp.float32
BF16 = jnp.bfloat16

EPS = 1e-6
FFN_HALF = 0.5
FFN_CHUNK = 256
MLA_SUB_TILES = 2
POOL_FFN_CHUNK = 256
POOL_WINDOWS = (2, 4, 8, 16)
POOL_HALO = 16
QK_NOPE = 128
QK_ROPE = 64
QK_HEAD = QK_NOPE + QK_ROPE
QK_PAD = 256
V_HEAD = 128
ROPE_THETA = 10000.0
LOG2E = math.log2(math.e)
Q_SCALE = QK_HEAD ** -0.5 * LOG2E
SUBLANES = 8
F32_SAFE_EXP2 = 120.0

VMEM_LIMIT = 56 * 1024 * 1024


def _cparams(sem):
    return pltpu.CompilerParams(dimension_semantics=sem, vmem_limit_bytes=VMEM_LIMIT)


def _resident(shape):
    nd = len(shape)
    return pl.BlockSpec(shape, lambda *_: (0,) * nd, pipeline_mode=pl.Buffered(1))


def _rms_rows(x, gain):
    ms = jnp.mean(x * x, axis=-1, keepdims=True)
    return x * lax.rsqrt(ms + EPS) * gain


def _swiglu_residual(x, g_ref, wg_ref, wu_ref, wd_ref, chunk=FFN_CHUNK, side_work=None):
    hn = _rms_rows(x, g_ref[...]).astype(BF16)
    d_ff = wg_ref.shape[1]
    y = None
    for c0 in range(0, d_ff, chunk):
        c1 = min(c0 + chunk, d_ff)
        g = jnp.dot(hn, wg_ref[:, c0:c1], preferred_element_type=F32)
        u = jnp.dot(hn, wu_ref[:, c0:c1], preferred_element_type=F32)
        a = (g / (1.0 + jnp.exp(-g)) * u).astype(BF16)
        yc = jnp.dot(a, wd_ref[c0:c1, :], preferred_element_type=F32)
        y = yc if y is None else y + yc
        if side_work is not None:
            next(side_work, None)
    if side_work is not None:
        for _ in side_work:
            pass
    return x + FFN_HALF * y


def _ffn_kernel(x_ref, g_ref, wg_ref, wu_ref, wd_ref, o_ref):
    o_ref[...] = _swiglu_residual(x_ref[...], g_ref, wg_ref, wu_ref, wd_ref)


def _proj_ffn_kernel(x_ref, a_ref, wo_ref, g_ref, wg_ref, wu_ref, wd_ref, o_ref):
    x = x_ref[...] + jnp.dot(a_ref[...], wo_ref[...], preferred_element_type=F32)
    o_ref[...] = _swiglu_residual(x, g_ref, wg_ref, wu_ref, wd_ref)


def _ffn(x2, gain, wg, wu, wd, tm, attn_out=None, w_out=None):
    T, D = x2.shape
    F = wg.shape[1]
    row_tile = pl.BlockSpec((tm, D), lambda i: (i, 0))
    ffn_specs = [_resident((1, D)), _resident((D, F)), _resident((D, F)), _resident((F, D))]
    ffn_args = (gain.reshape(1, D), wg, wu, wd)
    if attn_out is None:
        body, specs, args = _ffn_kernel, [row_tile] + ffn_specs, (x2,) + ffn_args
    else:
        body = _proj_ffn_kernel
        specs = [row_tile, pl.BlockSpec((tm, attn_out.shape[1]), lambda i: (i, 0)), _resident(w_out.shape)] + ffn_specs
        args = (x2, attn_out, w_out) + ffn_args
    return pl.pallas_call(
        body,
        grid=(T // tm,),
        in_specs=specs,
        out_specs=row_tile,
        out_shape=jax.ShapeDtypeStruct((T, D), F32),
        compiler_params=_cparams(("parallel",)),
        name="ffn" if attn_out is None else "proj_ffn",
    )(*args)


def _pool_mix(x, halo, first_of_seq, row0, g_ref, w_ref, sc_ref, hbuf, dst_ref):
    tm = x.shape[0]
    gain = g_ref[...]
    hn = _rms_rows(x, gain)
    hh = _rms_rows(halo, gain)
    hbuf[0:POOL_HALO] = jnp.where(first_of_seq, 0.0, hh)
    hbuf[POOL_HALO:POOL_HALO + tm] = hn
    yield
    C = w_ref.shape[1]
    rows = tm // 2
    for g, w_len in enumerate(POOL_WINDOWS):
        cols = slice(g * C, (g + 1) * C)
        for r0 in (0, rows):
            u = hbuf[POOL_HALO + r0:POOL_HALO + r0 + rows, cols]
            wsum = u
            for j in range(1, w_len):
                wsum = wsum + hbuf[POOL_HALO + r0 - j:POOL_HALO + r0 - j + rows, cols]
            t = row0 + r0 + lax.broadcasted_iota(jnp.int32, (rows, 1), 0)
            count = jnp.minimum(t + 1, w_len).astype(F32)
            pooled = (wsum / count - u).astype(BF16)
            y = jnp.dot(pooled, w_ref[g], preferred_element_type=F32)
            dst_ref[r0:r0 + rows, cols] = x[r0:r0 + rows, cols] + y * sc_ref[:, cols]
            yield


def _pool_ffn_kernel(x_ref, halo_ref, gmix_ref, pw_ref, psc_ref, g_ref, wg_ref, wu_ref, wd_ref, o_ref,
                     hbuf, xm0_ref, xm1_ref, *, n_tiles, tiles_per_seq):
    s = pl.program_id(0)
    tm = x_ref.shape[0]

    def mix_into(dst_ref):
        i = s % tiles_per_seq
        return _pool_mix(x_ref[...], halo_ref[...], i == 0, i * tm, gmix_ref, pw_ref, psc_ref, hbuf, dst_ref)

    def ffn_from(src_ref, side_work=None):
        o_ref[...] = _swiglu_residual(src_ref[...], g_ref, wg_ref, wu_ref, wd_ref, chunk=POOL_FFN_CHUNK,
                                      side_work=side_work)

    last_ref = xm0_ref if (n_tiles - 1) % 2 == 0 else xm1_ref
    middle = jnp.logical_and(s > 0, s < n_tiles)

    @pl.when(s == 0)
    def _():
        for _ in mix_into(xm0_ref):
            pass

    @pl.when(jnp.logical_and(middle, s % 2 == 1))
    def _():
        ffn_from(xm0_ref, mix_into(xm1_ref))

    @pl.when(jnp.logical_and(middle, s % 2 == 0))
    def _():
        ffn_from(xm1_ref, mix_into(xm0_ref))

    @pl.when(s == n_tiles)
    def _():
        ffn_from(last_ref)


def _pool_ffn(x2, seq_len, gmix, pool_w, pool_scale, gain, wg, wu, wd, tm):
    T, D = x2.shape
    F = wg.shape[1]
    G, C, _ = pool_w.shape
    n_tiles = T // tm
    hb = tm // POOL_HALO
    kern = functools.partial(_pool_ffn_kernel, n_tiles=n_tiles, tiles_per_seq=seq_len // tm)
    return pl.pallas_call(
        kern,
        grid=(n_tiles + 1,),
        in_specs=[
            pl.BlockSpec((tm, D), lambda s: (jnp.minimum(s, n_tiles - 1), 0)),
            pl.BlockSpec((POOL_HALO, D), lambda s: (jnp.maximum(jnp.minimum(s, n_tiles - 1) * hb - 1, 0), 0)),
            _resident((1, D)),
            _resident((G, C, C)),
            _resident((1, D)),
            _resident((1, D)),
            _resident((D, F)),
            _resident((D, F)),
            _resident((F, D)),
        ],
        out_specs=pl.BlockSpec((tm, D), lambda s: (jnp.maximum(s - 1, 0), 0)),
        out_shape=jax.ShapeDtypeStruct((T, D), F32),
        scratch_shapes=[pltpu.VMEM((POOL_HALO + tm, D), F32), pltpu.VMEM((tm, D), F32), pltpu.VMEM((tm, D), F32)],
        compiler_params=_cparams(("arbitrary",)),
        name="pool_ffn",
    )(x2, x2, gmix.reshape(1, D), pool_w, pool_scale.reshape(1, D), gain.reshape(1, D), wg, wu, wd)


def _rms_cols(xT, gain_col, n):
    ms = jnp.sum(xT * xT, axis=0, keepdims=True) * (1.0 / n)
    return xT * lax.rsqrt(ms + EPS) * gain_col


def _mla_pre_kernel(bound_ref, x_ref, pos_ref, invf_ref, gmix_ref, winT_ref, gq_ref, wqT_ref, gkv_ref, wkvT_ref,
                    gqh_ref, gkh_ref, qT_ref, k_ref, vT_ref, *, n_heads, q_lora, kv_lora, q_scale):
    tm = x_ref.shape[1]
    half = QK_ROPE // 2
    kv_head = QK_NOPE + V_HEAD

    def sub_tile(t0, tn):
        tok = slice(t0, t0 + tn)
        hn = _rms_rows(x_ref[0, tok, :], gmix_ref[...]).astype(BF16)
        latT = lax.dot_general(winT_ref[...], hn, (((1,), (1,)), ((), ())), preferred_element_type=F32)
        yield
        kpeT = latT[q_lora + kv_lora:]
        cqn = _rms_cols(latT[:q_lora], gq_ref[...], q_lora).astype(BF16)
        ckvn = _rms_cols(latT[q_lora:q_lora + kv_lora], gkv_ref[...], kv_lora).astype(BF16)
        ang = pos_ref[0, :, tok].astype(F32) * invf_ref[...]
        cos, sin = jnp.cos(ang), jnp.sin(ang)
        gqh = gqh_ref[...]
        gkh = gkh_ref[...]
        kpe_ss = jnp.sum(kpeT * kpeT, axis=0, keepdims=True)
        first = lax.broadcasted_iota(jnp.int32, (QK_PAD - QK_HEAD, tn), 0) == 0
        q_pad = jnp.where(first, 1.0, 0.0)
        k_pad = jnp.where(first, -bound_ref[0], 0.0)

        def rope(x1, x2):
            return x1 * cos - x2 * sin, x2 * cos + x1 * sin

        yield
        for h in range(n_heads):
            qh = jnp.dot(wqT_ref[h * QK_HEAD:(h + 1) * QK_HEAD, :], cqn, preferred_element_type=F32)
            kvh = jnp.dot(wkvT_ref[h * kv_head:(h + 1) * kv_head, :], ckvn, preferred_element_type=F32)
            ms = jnp.sum(qh * qh, axis=0, keepdims=True) * (1.0 / QK_HEAD)
            qn = qh * (lax.rsqrt(ms + EPS) * q_scale) * gqh
            r1, r2 = rope(qn[QK_NOPE:QK_NOPE + half], qn[QK_NOPE + half:])
            qT_ref[0, h, :, tok] = jnp.concatenate([qn[:QK_NOPE], r1, r2, q_pad], axis=0).astype(BF16)

            kn = kvh[:QK_NOPE]
            ms = (jnp.sum(kn * kn, axis=0, keepdims=True) + kpe_ss) * (1.0 / QK_HEAD)
            rs = lax.rsqrt(ms + EPS)
            r1, r2 = rope(kpeT[:half] * rs * gkh[QK_NOPE:QK_NOPE + half],
                          kpeT[half:] * rs * gkh[QK_NOPE + half:])
            k_ref[0, h, 0, tok, :QK_NOPE] = (kn * rs * gkh[:QK_NOPE]).T.astype(BF16)
            k_ref[0, h, 0, tok, QK_NOPE:] = jnp.concatenate([r1, r2, k_pad], axis=0).T.astype(BF16)
            vT_ref[0, h, 0, :, tok] = kvh[QK_NOPE:].astype(BF16)
            yield

    n_sub = MLA_SUB_TILES if tm % (MLA_SUB_TILES * 128) == 0 else 1
    stages = [sub_tile(i * (tm // n_sub), tm // n_sub) for i in range(n_sub)]
    live = list(stages)
    lead = 0
    while live:
        lead += 1
        for g in list(live[:lead]):
            if next(g, StopIteration) is StopIteration:
                live.remove(g)
                lead -= 1


def _mla_pre(bound, x3, pos3, invf, gmix, winT, gq, wqT, gkv, wkvT, gqh, gkh, n_heads, tm):
    B, S, D = x3.shape
    q_lora, kv_lora = wqT.shape[1], wkvT.shape[1]
    n_t = S // tm
    kern = functools.partial(_mla_pre_kernel, n_heads=n_heads, q_lora=q_lora, kv_lora=kv_lora,
                             q_scale=Q_SCALE)
    return pl.pallas_call(
        kern,
        grid=(B, n_t),
        in_specs=[
            pl.BlockSpec(memory_space=pltpu.SMEM),
            pl.BlockSpec((1, tm, D), lambda b, i: (b, i, 0)),
            pl.BlockSpec((1, 1, tm), lambda b, i: (b, 0, i)),
            _resident(invf.shape),
            _resident((1, D)),
            _resident(winT.shape),
            _resident((q_lora, 1)),
            _resident(wqT.shape),
            _resident((kv_lora, 1)),
            _resident(wkvT.shape),
            _resident((QK_HEAD, 1)),
            _resident((QK_HEAD, 1)),
        ],
        out_specs=[
            pl.BlockSpec((1, n_heads, QK_PAD, tm), lambda b, i: (b, 0, 0, i)),
            pl.BlockSpec((1, n_heads, 1, tm, QK_PAD), lambda b, i: (b, 0, i, 0, 0)),
            pl.BlockSpec((1, n_heads, 1, V_HEAD, tm), lambda b, i: (b, 0, i, 0, 0)),
        ],
        out_shape=[
            jax.ShapeDtypeStruct((B, n_heads, QK_PAD, S), BF16),
            jax.ShapeDtypeStruct((B, n_heads, n_t, tm, QK_PAD), BF16),
            jax.ShapeDtypeStruct((B, n_heads, n_t, V_HEAD, tm), BF16),
        ],
        compiler_params=_cparams(("parallel", "parallel")),
        name="mla_pre",
    )(bound, x3, pos3, invf, gmix.reshape(1, D), winT, gq.reshape(-1, 1), wqT, gkv.reshape(-1, 1), wkvT,
      gqh.reshape(-1, 1), gkh.reshape(-1, 1))


def _attn_kernel(safe_ref, qT_ref, k_ref, vT_ref, o_ref, m_ref, l_ref, acc_ref, p0_ref, p1_ref):
    i = pl.program_id(2)
    sub = l_ref.shape[0]
    tk, tq = p0_ref.shape
    l_ref[...] = jnp.zeros(l_ref.shape, F32)
    acc_ref[...] = jnp.zeros(acc_ref.shape, F32)

    def scores(j, diag=None):
        sT = jnp.dot(k_ref[0, 0, j], qT_ref[0, 0], preferred_element_type=F32)
        if diag is None:
            return sT
        key = lax.broadcasted_iota(jnp.int32, sT.shape, 0) + diag * tk
        qry = lax.broadcasted_iota(jnp.int32, sT.shape, 1)
        return jnp.where(key <= qry, sT, -jnp.inf)

    def partial_sum(p):
        return jnp.sum(p.reshape(p.shape[0] // sub, sub, p.shape[1]), axis=0)

    def probs(j, p_ref, diag=None):
        p = jnp.exp2(scores(j, diag))
        l_ref[...] += partial_sum(p)
        p_ref[...] = p.astype(BF16)

    def probs_last_diag(j, p_ref):
        sT = jnp.dot(k_ref[0, 0, j], qT_ref[0, 0, :, tk:], preferred_element_type=F32)
        key = lax.broadcasted_iota(jnp.int32, sT.shape, 0)
        qry = lax.broadcasted_iota(jnp.int32, sT.shape, 1)
        p = jnp.exp2(jnp.where(key <= qry, sT, -jnp.inf))
        l_ref[:, tk:] += partial_sum(p)
        p_ref[:, :tk] = jnp.zeros((tk, tk), BF16)
        p_ref[:, tk:] = p.astype(BF16)

    def pv(j, p_ref):
        acc_ref[...] += jnp.dot(vT_ref[0, 0, j], p_ref[...], preferred_element_type=F32)

    def online(j, diag=None):
        sT = scores(j, diag)
        m_old = m_ref[...]
        m_new = jnp.maximum(m_old, jnp.max(sT, axis=0, keepdims=True))
        alpha = jnp.exp2(m_old - m_new)
        p = jnp.exp2(sT - m_new)
        l_ref[...] = alpha * l_ref[...] + partial_sum(p)
        acc_ref[...] = alpha * acc_ref[...] + jnp.dot(vT_ref[0, 0, j], p.astype(BF16),
                                                      preferred_element_type=F32)
        m_ref[...] = m_new

    @pl.when(safe_ref[0] != 0)
    def _():
        probs(2 * i, p0_ref, diag=0)
        probs_last_diag(2 * i + 1, p1_ref)
        pv(2 * i, p0_ref)

        def pair(t):
            probs(2 * t, p0_ref)
            pv(jnp.where(t == 0, 2 * i + 1, 2 * t - 1), p1_ref)
            probs(2 * t + 1, p1_ref)
            pv(2 * t, p0_ref)

        def two_pairs(t, carry):
            pair(2 * t)
            pair(2 * t + 1)
            return carry

        lax.fori_loop(0, i // 2, two_pairs, 0)

        @pl.when(i % 2 == 1)
        def _():
            pair(i - 1)

        pv(jnp.where(i == 0, 1, 2 * i - 1), p1_ref)

    @pl.when(safe_ref[0] == 0)
    def _():
        m_ref[...] = jnp.full(m_ref.shape, -jnp.inf, F32)

        def body(j, carry):
            online(j)
            return carry

        lax.fori_loop(0, 2 * i, body, 0)
        online(2 * i, diag=0)
        online(2 * i + 1, diag=1)

    l = jnp.sum(l_ref[...], axis=0, keepdims=True)
    o_ref[0] = (acc_ref[...] / l).T.astype(o_ref.dtype)


def _attn(safe, qT, k, vT):
    B, H, _, S = qT.shape
    n_k, tk = k.shape[2], k.shape[3]
    tq = 2 * tk
    assert S % tq == 0, (S, tq)
    return pl.pallas_call(
        _attn_kernel,
        grid=(B, H, S // tq),
        in_specs=[
            pl.BlockSpec(memory_space=pltpu.SMEM),
            pl.BlockSpec((1, 1, QK_PAD, tq), lambda b, h, i: (b, h, 0, i)),
            pl.BlockSpec((1, 1, n_k, tk, QK_PAD), lambda b, h, i: (b, h, 0, 0, 0)),
            pl.BlockSpec((1, 1, n_k, V_HEAD, tk), lambda b, h, i: (b, h, 0, 0, 0)),
        ],
        out_specs=pl.BlockSpec((1, tq, V_HEAD), lambda b, h, i: (b, i, h)),
        out_shape=jax.ShapeDtypeStruct((B, S, H * V_HEAD), BF16),
        scratch_shapes=[pltpu.VMEM((1, tq), F32), pltpu.VMEM((SUBLANES, tq), F32),
                        pltpu.VMEM((V_HEAD, tq), F32), pltpu.VMEM((tk, tq), BF16), pltpu.VMEM((tk, tq), BF16)],
        compiler_params=_cparams(("parallel", "parallel", "arbitrary")),
        name="attn",
    )(safe, qT, k, vT)


def _tile(n, want):
    t = min(n, want)
    assert n % t == 0, (n, t)
    return t


def kernel(x, positions, ffn1_norm, ffn1_w_gate, ffn1_w_up, ffn1_w_down, mix_norm, pool_w, pool_scale,
           mla_w_in, mla_q_norm, mla_w_q_up, mla_kv_norm, mla_w_kv_up, mla_q_head_norm, mla_k_head_norm,
           mla_w_out, ffn2_norm, ffn2_w_gate, ffn2_w_up, ffn2_w_down):
    B, S, D = x.shape
    depth = ffn1_norm.shape[0]
    n_heads = mla_w_out.shape[1] // V_HEAD
    T = B * S
    tm_ffn = _tile(T, 1024)
    tm_pool = _tile(S, 512)
    tk = _tile(S // 2, 512)

    invf = (1.0 / (ROPE_THETA ** (jnp.arange(0, QK_ROPE, 2, dtype=F32) / QK_ROPE))).reshape(-1, 1)
    pos3 = positions.reshape(B, 1, S)

    for i in range(depth):
        x = _ffn(x.reshape(T, D), ffn1_norm[i], ffn1_w_gate[i].astype(BF16), ffn1_w_up[i].astype(BF16),
                 ffn1_w_down[i].astype(BF16), tm_ffn).reshape(B, S, D)
        j = i // 2
        ffn2 = (ffn2_norm[i], ffn2_w_gate[i].astype(BF16), ffn2_w_up[i].astype(BF16), ffn2_w_down[i].astype(BF16))
        if i % 2 == 0:
            x = _pool_ffn(x.reshape(T, D), S, mix_norm[i], pool_w[j].astype(BF16), pool_scale[j], *ffn2,
                          tm_pool).reshape(B, S, D)
        else:
            bound = (Q_SCALE * QK_HEAD * jnp.max(jnp.abs(mla_q_head_norm[j]))
                     * jnp.max(jnp.abs(mla_k_head_norm[j]))).reshape(1)
            safe = (2.0 * bound <= F32_SAFE_EXP2).astype(jnp.int32)
            qT, k, vT = _mla_pre(bound, x, pos3, invf, mix_norm[i], mla_w_in[j].T.astype(BF16), mla_q_norm[j],
                                 mla_w_q_up[j].T.astype(BF16), mla_kv_norm[j], mla_w_kv_up[j].T.astype(BF16),
                                 mla_q_head_norm[j], mla_k_head_norm[j], n_heads, tk)
            o = _attn(safe, qT, k, vT)
            x = _ffn(x.reshape(T, D), *ffn2, tm_ffn, attn_out=o.reshape(T, n_heads * V_HEAD),
                     w_out=mla_w_out[j].astype(BF16)).reshape(B, S, D)
    return x
```

```python
import functools
import itertools
import math

import jax
import jax.numpy as jnp
from jax import lax
from jax.experimental import pallas as pl
from jax.experimental.pallas import tpu as pltpu

F32 = jnp.float32
BF16 = jnp.bfloat16

EPS = 1e-6
FFN_HALF = 0.5
FFN_CHUNK = 256
MLA_SUB_TILES = 2
POOL_FFN_CHUNK = 256
POOL_WINDOWS = (2, 4, 8, 16)
POOL_HALO = 16
QK_NOPE = 128
QK_ROPE = 64
QK_HEAD = QK_NOPE + QK_ROPE
QK_PAD = 256
V_HEAD = 128
ROPE_THETA = 10000.0
LOG2E = math.log2(math.e)
Q_SCALE = QK_HEAD ** -0.5 * LOG2E
SUBLANES = 8
BF16_ROWS = 16
F32_SAFE_EXP2 = 120.0

VMEM_LIMIT = 56 * 1024 * 1024


def _cparams(sem):
    return pltpu.CompilerParams(dimension_semantics=sem, vmem_limit_bytes=VMEM_LIMIT)


def _resident(shape):
    nd = len(shape)
    return pl.BlockSpec(shape, lambda *_: (0,) * nd, pipeline_mode=pl.Buffered(1))


def _rms_rows(x, gain):
    ms = jnp.mean(x * x, axis=-1, keepdims=True)
    return x * lax.rsqrt(ms + EPS) * gain


def _swiglu_residual(x, g_ref, wg_ref, wu_ref, wd_ref, chunk=FFN_CHUNK, side_work=None):
    hn = _rms_rows(x, g_ref[...]).astype(BF16)
    d_ff = wg_ref.shape[1]
    y = None
    for c0 in range(0, d_ff, chunk):
        c1 = min(c0 + chunk, d_ff)
        g = jnp.dot(hn, wg_ref[:, c0:c1], preferred_element_type=F32)
        u = jnp.dot(hn, wu_ref[:, c0:c1], preferred_element_type=F32)
        a = (g / (1.0 + jnp.exp(-g)) * u).astype(BF16)
        yc = jnp.dot(a, wd_ref[c0:c1, :], preferred_element_type=F32)
        y = yc if y is None else y + yc
        if side_work is not None:
            next(side_work, None)
    if side_work is not None:
        for _ in side_work:
            pass
    return x + FFN_HALF * y


def _cast_pieces(src_refs, dst_refs):
    for src, dst in zip(src_refs, dst_refs):
        dst[...] = src[...].astype(BF16)
        yield


def _cast_plan(next_w, n_steps):
    if next_w is None:
        return [], [], [], ()
    stacked, layer = next_w
    in_specs, out_specs, out_shapes = [], [], []
    for w in stacked:
        _, rows, cols = w.shape
        rb = next(r for r in range(BF16_ROWS, rows + 1, BF16_ROWS) if rows % r == 0 and rows // r <= n_steps)
        last = rows // rb - 1
        in_specs.append(pl.BlockSpec((None, rb, cols), lambda s, last=last: (layer, jnp.minimum(s, last), 0)))
        out_specs.append(pl.BlockSpec((rb, cols), lambda s, last=last: (jnp.minimum(s, last), 0)))
        out_shapes.append(jax.ShapeDtypeStruct((rows, cols), BF16))
    return in_specs, out_specs, out_shapes, tuple(stacked)


def _ffn_kernel(*refs, n_cast, with_proj):
    n_in = (7 if with_proj else 5) + n_cast
    ins, o_ref, cast_out = refs[:n_in - n_cast], refs[n_in], refs[n_in + 1:]
    cast_in = refs[n_in - n_cast:n_in]
    if with_proj:
        x_ref, a_ref, wo_ref, g_ref, wg_ref, wu_ref, wd_ref = ins
        x = x_ref[...] + jnp.dot(a_ref[...], wo_ref[...], preferred_element_type=F32)
    else:
        x_ref, g_ref, wg_ref, wu_ref, wd_ref = ins
        x = x_ref[...]
    o_ref[...] = _swiglu_residual(x, g_ref, wg_ref, wu_ref, wd_ref, side_work=_cast_pieces(cast_in, cast_out))


def _ffn(x2, gain, wg, wu, wd, tm, attn_out=None, w_out=None, next_w=None):
    T, D = x2.shape
    F = wg.shape[1]
    n_steps = T // tm
    row_tile = pl.BlockSpec((tm, D), lambda i: (i, 0))
    specs = [row_tile]
    args = (x2,)
    if attn_out is not None:
        specs += [pl.BlockSpec((tm, attn_out.shape[1]), lambda i: (i, 0)), _resident(w_out.shape)]
        args += (attn_out, w_out)
    specs += [_resident((1, D)), _resident((D, F)), _resident((D, F)), _resident((F, D))]
    args += (gain.reshape(1, D), wg, wu, wd)
    cast_in, cast_out, cast_shapes, cast_args = _cast_plan(next_w, n_steps)
    out = pl.pallas_call(
        functools.partial(_ffn_kernel, n_cast=len(cast_args), with_proj=attn_out is not None),
        grid=(n_steps,),
        in_specs=specs + cast_in,
        out_specs=[row_tile] + cast_out,
        out_shape=[jax.ShapeDtypeStruct((T, D), F32)] + cast_shapes,
        compiler_params=_cparams(("arbitrary",)),
        name="ffn" if attn_out is None else "proj_ffn",
    )(*args, *cast_args)
    return out[0], tuple(out[1:])


def _pool_mix(x, halo, first_of_seq, row0, g_ref, w_ref, sc_ref, hbuf, dst_ref):
    tm = x.shape[0]
    gain = g_ref[...]
    hn = _rms_rows(x, gain)
    hh = _rms_rows(halo, gain)
    hbuf[0:POOL_HALO] = jnp.where(first_of_seq, 0.0, hh)
    hbuf[POOL_HALO:POOL_HALO + tm] = hn
    yield
    C = w_ref.shape[1]
    rows = tm // 2
    for g, w_len in enumerate(POOL_WINDOWS):
        cols = slice(g * C, (g + 1) * C)
        for r0 in (0, rows):
            u = hbuf[POOL_HALO + r0:POOL_HALO + r0 + rows, cols]
            wsum = u
            for j in range(1, w_len):
                wsum = wsum + hbuf[POOL_HALO + r0 - j:POOL_HALO + r0 - j + rows, cols]
            t = row0 + r0 + lax.broadcasted_iota(jnp.int32, (rows, 1), 0)
            count = jnp.minimum(t + 1, w_len).astype(F32)
            pooled = (wsum / count - u).astype(BF16)
            y = jnp.dot(pooled, w_ref[g], preferred_element_type=F32)
            dst_ref[r0:r0 + rows, cols] = x[r0:r0 + rows, cols] + y * sc_ref[:, cols]
            yield


def _pool_ffn_kernel(*refs, n_tiles, tiles_per_seq, n_cast):
    x_ref, halo_ref, gmix_ref, pw_ref, psc_ref, g_ref, wg_ref, wu_ref, wd_ref = refs[:9]
    cast_in, o_ref, cast_out = refs[9:9 + n_cast], refs[9 + n_cast], refs[10 + n_cast:10 + 2 * n_cast]
    hbuf, xm0_ref, xm1_ref = refs[10 + 2 * n_cast:]
    s = pl.program_id(0)
    tm = x_ref.shape[0]

    def side(dst_ref=None):
        casts = _cast_pieces(cast_in, cast_out)
        if dst_ref is None:
            return casts
        i = s % tiles_per_seq
        mix = _pool_mix(x_ref[...], halo_ref[...], i == 0, i * tm, gmix_ref, pw_ref, psc_ref, hbuf, dst_ref)
        return itertools.chain(mix, casts)

    def ffn_from(src_ref, side_work):
        o_ref[...] = _swiglu_residual(src_ref[...], g_ref, wg_ref, wu_ref, wd_ref, chunk=POOL_FFN_CHUNK,
                                      side_work=side_work)

    last_ref = xm0_ref if (n_tiles - 1) % 2 == 0 else xm1_ref
    middle = jnp.logical_and(s > 0, s < n_tiles)

    @pl.when(s == 0)
    def _():
        for _ in side(xm0_ref):
            pass

    @pl.when(jnp.logical_and(middle, s % 2 == 1))
    def _():
        ffn_from(xm0_ref, side(xm1_ref))

    @pl.when(jnp.logical_and(middle, s % 2 == 0))
    def _():
        ffn_from(xm1_ref, side(xm0_ref))

    @pl.when(s == n_tiles)
    def _():
        ffn_from(last_ref, side())


def _pool_ffn(x2, seq_len, gmix, pool_w, pool_scale, gain, wg, wu, wd, tm, next_w=None):
    T, D = x2.shape
    F = wg.shape[1]
    G, C, _ = pool_w.shape
    n_tiles = T // tm
    hb = tm // POOL_HALO
    cast_in, cast_out, cast_shapes, cast_args = _cast_plan(next_w, n_tiles + 1)
    kern = functools.partial(_pool_ffn_kernel, n_tiles=n_tiles, tiles_per_seq=seq_len // tm, n_cast=len(cast_args))
    out = pl.pallas_call(
        kern,
        grid=(n_tiles + 1,),
        in_specs=[
            pl.BlockSpec((tm, D), lambda s: (jnp.minimum(s, n_tiles - 1), 0)),
            pl.BlockSpec((POOL_HALO, D), lambda s: (jnp.maximum(jnp.minimum(s, n_tiles - 1) * hb - 1, 0), 0)),
            _resident((1, D)),
            _resident((G, C, C)),
            _resident((1, D)),
            _resident((1, D)),
            _resident((D, F)),
            _resident((D, F)),
            _resident((F, D)),
        ] + cast_in,
        out_specs=[pl.BlockSpec((tm, D), lambda s: (jnp.maximum(s - 1, 0), 0))] + cast_out,
        out_shape=[jax.ShapeDtypeStruct((T, D), F32)] + cast_shapes,
        scratch_shapes=[pltpu.VMEM((POOL_HALO + tm, D), F32), pltpu.VMEM((tm, D), F32), pltpu.VMEM((tm, D), F32)],
        compiler_params=_cparams(("arbitrary",)),
        name="pool_ffn",
    )(x2, x2, gmix.reshape(1, D), pool_w, pool_scale.reshape(1, D), gain.reshape(1, D), wg, wu, wd, *cast_args)
    return out[0], tuple(out[1:])


def _rms_cols(xT, gain_col, n):
    ms = jnp.sum(xT * xT, axis=0, keepdims=True) * (1.0 / n)
    return xT * lax.rsqrt(ms + EPS) * gain_col


def _mla_pre_kernel(bound_ref, x_ref, pos_ref, invf_ref, gmix_ref, winT_ref, gq_ref, wqT_ref, gkv_ref, wkvT_ref,
                    gqh_ref, gkh_ref, qT_ref, k_ref, vT_ref, *, n_heads, q_lora, kv_lora, q_scale):
    tm = x_ref.shape[1]
    half = QK_ROPE // 2
    kv_head = QK_NOPE + V_HEAD

    def sub_tile(t0, tn):
        tok = slice(t0, t0 + tn)
        hn = _rms_rows(x_ref[0, tok, :], gmix_ref[...]).astype(BF16)
        latT = lax.dot_general(winT_ref[...], hn, (((1,), (1,)), ((), ())), preferred_element_type=F32)
        yield
        kpeT = latT[q_lora + kv_lora:]
        cqn = _rms_cols(latT[:q_lora], gq_ref[...], q_lora).astype(BF16)
        ckvn = _rms_cols(latT[q_lora:q_lora + kv_lora], gkv_ref[...], kv_lora).astype(BF16)
        ang = pos_ref[0, :, tok].astype(F32) * invf_ref[...]
        cos, sin = jnp.cos(ang), jnp.sin(ang)
        gqh = gqh_ref[...]
        gkh = gkh_ref[...]
        kpe_ss = jnp.sum(kpeT * kpeT, axis=0, keepdims=True)
        first = lax.broadcasted_iota(jnp.int32, (QK_PAD - QK_HEAD, tn), 0) == 0
        q_pad = jnp.where(first, 1.0, 0.0)
        k_pad = jnp.where(first, -bound_ref[0], 0.0)

        def rope(x1, x2):
            return x1 * cos - x2 * sin, x2 * cos + x1 * sin

        yield
        for h in range(n_heads):
            qh = jnp.dot(wqT_ref[h * QK_HEAD:(h + 1) * QK_HEAD, :], cqn, preferred_element_type=F32)
            kvh = jnp.dot(wkvT_ref[h * kv_head:(h + 1) * kv_head, :], ckvn, preferred_element_type=F32)
            ms = jnp.sum(qh * qh, axis=0, keepdims=True) * (1.0 / QK_HEAD)
            qn = qh * (lax.rsqrt(ms + EPS) * q_scale) * gqh
            r1, r2 = rope(qn[QK_NOPE:QK_NOPE + half], qn[QK_NOPE + half:])
            qT_ref[0, h, :, tok] = jnp.concatenate([qn[:QK_NOPE], r1, r2, q_pad], axis=0).astype(BF16)

            kn = kvh[:QK_NOPE]
            ms = (jnp.sum(kn * kn, axis=0, keepdims=True) + kpe_ss) * (1.0 / QK_HEAD)
            rs = lax.rsqrt(ms + EPS)
            r1, r2 = rope(kpeT[:half] * rs * gkh[QK_NOPE:QK_NOPE + half],
                          kpeT[half:] * rs * gkh[QK_NOPE + half:])
            k_ref[0, h, 0, tok, :QK_NOPE] = (kn * rs * gkh[:QK_NOPE]).T.astype(BF16)
            k_ref[0, h, 0, tok, QK_NOPE:] = jnp.concatenate([r1, r2, k_pad], axis=0).T.astype(BF16)
            vT_ref[0, h, 0, :, tok] = kvh[QK_NOPE:].astype(BF16)
            yield

    n_sub = MLA_SUB_TILES if tm % (MLA_SUB_TILES * 128) == 0 else 1
    stages = [sub_tile(i * (tm // n_sub), tm // n_sub) for i in range(n_sub)]
    live = list(stages)
    lead = 0
    while live:
        lead += 1
        for g in list(live[:lead]):
            if next(g, StopIteration) is StopIteration:
                live.remove(g)
                lead -= 1


def _mla_pre(bound, x3, pos3, invf, gmix, winT, gq, wqT, gkv, wkvT, gqh, gkh, n_heads, tm):
    B, S, D = x3.shape
    q_lora, kv_lora = wqT.shape[1], wkvT.shape[1]
    n_t = S // tm
    kern = functools.partial(_mla_pre_kernel, n_heads=n_heads, q_lora=q_lora, kv_lora=kv_lora,
                             q_scale=Q_SCALE)
    return pl.pallas_call(
        kern,
        grid=(B, n_t),
        in_specs=[
            pl.BlockSpec(memory_space=pltpu.SMEM),
            pl.BlockSpec((1, tm, D), lambda b, i: (b, i, 0)),
            pl.BlockSpec((1, 1, tm), lambda b, i: (b, 0, i)),
            _resident(invf.shape),
            _resident((1, D)),
            _resident(winT.shape),
            _resident((q_lora, 1)),
            _resident(wqT.shape),
            _resident((kv_lora, 1)),
            _resident(wkvT.shape),
            _resident((QK_HEAD, 1)),
            _resident((QK_HEAD, 1)),
        ],
        out_specs=[
            pl.BlockSpec((1, n_heads, QK_PAD, tm), lambda b, i: (b, 0, 0, i)),
            pl.BlockSpec((1, n_heads, 1, tm, QK_PAD), lambda b, i: (b, 0, i, 0, 0)),
            pl.BlockSpec((1, n_heads, 1, V_HEAD, tm), lambda b, i: (b, 0, i, 0, 0)),
        ],
        out_shape=[
            jax.ShapeDtypeStruct((B, n_heads, QK_PAD, S), BF16),
            jax.ShapeDtypeStruct((B, n_heads, n_t, tm, QK_PAD), BF16),
            jax.ShapeDtypeStruct((B, n_heads, n_t, V_HEAD, tm), BF16),
        ],
        compiler_params=_cparams(("parallel", "parallel")),
        name="mla_pre",
    )(bound, x3, pos3, invf, gmix.reshape(1, D), winT, gq.reshape(-1, 1), wqT, gkv.reshape(-1, 1), wkvT,
      gqh.reshape(-1, 1), gkh.reshape(-1, 1))


def _attn_kernel(safe_ref, qT_ref, k_ref, vT_ref, o_ref, m_ref, l_ref, acc_ref, p0_ref, p1_ref):
    i = pl.program_id(2)
    sub = l_ref.shape[0]
    tk, tq = p0_ref.shape
    l_ref[...] = jnp.zeros(l_ref.shape, F32)
    acc_ref[...] = jnp.zeros(acc_ref.shape, F32)

    def scores(j, diag=None):
        sT = jnp.dot(k_ref[0, 0, j], qT_ref[0, 0], preferred_element_type=F32)
        if diag is None:
            return sT
        key = lax.broadcasted_iota(jnp.int32, sT.shape, 0) + diag * tk
        qry = lax.broadcasted_iota(jnp.int32, sT.shape, 1)
        return jnp.where(key <= qry, sT, -jnp.inf)

    def partial_sum(p):
        return jnp.sum(p.reshape(p.shape[0] // sub, sub, p.shape[1]), axis=0)

    def probs(j, p_ref, diag=None):
        p = jnp.exp2(scores(j, diag))
        l_ref[...] += partial_sum(p)
        p_ref[...] = p.astype(BF16)

    def probs_last_diag(j, p_ref):
        sT = jnp.dot(k_ref[0, 0, j], qT_ref[0, 0, :, tk:], preferred_element_type=F32)
        key = lax.broadcasted_iota(jnp.int32, sT.shape, 0)
        qry = lax.broadcasted_iota(jnp.int32, sT.shape, 1)
        p = jnp.exp2(jnp.where(key <= qry, sT, -jnp.inf))
        l_ref[:, tk:] += partial_sum(p)
        p_ref[:, :tk] = jnp.zeros((tk, tk), BF16)
        p_ref[:, tk:] = p.astype(BF16)

    def pv(j, p_ref):
        acc_ref[...] += jnp.dot(vT_ref[0, 0, j], p_ref[...], preferred_element_type=F32)

    def online(j, diag=None):
        sT = scores(j, diag)
        m_old = m_ref[...]
        m_new = jnp.maximum(m_old, jnp.max(sT, axis=0, keepdims=True))
        alpha = jnp.exp2(m_old - m_new)
        p = jnp.exp2(sT - m_new)
        l_ref[...] = alpha * l_ref[...] + partial_sum(p)
        acc_ref[...] = alpha * acc_ref[...] + jnp.dot(vT_ref[0, 0, j], p.astype(BF16),
                                                      preferred_element_type=F32)
        m_ref[...] = m_new

    @pl.when(safe_ref[0] != 0)
    def _():
        probs(2 * i, p0_ref, diag=0)
        probs_last_diag(2 * i + 1, p1_ref)
        pv(2 * i, p0_ref)

        def pair(t):
            probs(2 * t, p0_ref)
            pv(jnp.where(t == 0, 2 * i + 1, 2 * t - 1), p1_ref)
            probs(2 * t + 1, p1_ref)
            pv(2 * t, p0_ref)

        def two_pairs(t, carry):
            pair(2 * t)
            pair(2 * t + 1)
            return carry

        lax.fori_loop(0, i // 2, two_pairs, 0)

        @pl.when(i % 2 == 1)
        def _():
            pair(i - 1)

        pv(jnp.where(i == 0, 1, 2 * i - 1), p1_ref)

    @pl.when(safe_ref[0] == 0)
    def _():
        m_ref[...] = jnp.full(m_ref.shape, -jnp.inf, F32)

        def body(j, carry):
            online(j)
            return carry

        lax.fori_loop(0, 2 * i, body, 0)
        online(2 * i, diag=0)
        online(2 * i + 1, diag=1)

    l = jnp.sum(l_ref[...], axis=0, keepdims=True)
    o_ref[0] = (acc_ref[...] / l).T.astype(o_ref.dtype)


def _attn(safe, qT, k, vT):
    B, H, _, S = qT.shape
    n_k, tk = k.shape[2], k.shape[3]
    tq = 2 * tk
    assert S % tq == 0, (S, tq)
    return pl.pallas_call(
        _attn_kernel,
        grid=(B, H, S // tq),
        in_specs=[
            pl.BlockSpec(memory_space=pltpu.SMEM),
            pl.BlockSpec((1, 1, QK_PAD, tq), lambda b, h, i: (b, h, 0, i)),
            pl.BlockSpec((1, 1, n_k, tk, QK_PAD), lambda b, h, i: (b, h, 0, 0, 0)),
            pl.BlockSpec((1, 1, n_k, V_HEAD, tk), lambda b, h, i: (b, h, 0, 0, 0)),
        ],
        out_specs=pl.BlockSpec((1, tq, V_HEAD), lambda b, h, i: (b, i, h)),
        out_shape=jax.ShapeDtypeStruct((B, S, H * V_HEAD), BF16),
        scratch_shapes=[pltpu.VMEM((1, tq), F32), pltpu.VMEM((SUBLANES, tq), F32),
                        pltpu.VMEM((V_HEAD, tq), F32), pltpu.VMEM((tk, tq), BF16), pltpu.VMEM((tk, tq), BF16)],
        compiler_params=_cparams(("parallel", "parallel", "arbitrary")),
        name="attn",
    )(safe, qT, k, vT)


def _tile(n, want):
    t = min(n, want)
    assert n % t == 0, (n, t)
    return t


def kernel(x, positions, ffn1_norm, ffn1_w_gate, ffn1_w_up, ffn1_w_down, mix_norm, pool_w, pool_scale,
           mla_w_in, mla_q_norm, mla_w_q_up, mla_kv_norm, mla_w_kv_up, mla_q_head_norm, mla_k_head_norm,
           mla_w_out, ffn2_norm, ffn2_w_gate, ffn2_w_up, ffn2_w_down):
    B, S, D = x.shape
    depth = ffn1_norm.shape[0]
    n_heads = mla_w_out.shape[1] // V_HEAD
    T = B * S
    tm_ffn = _tile(T, 1024)
    tm_pool = _tile(S, 512)
    tk = _tile(S // 2, 512)

    invf = (1.0 / (ROPE_THETA ** (jnp.arange(0, QK_ROPE, 2, dtype=F32) / QK_ROPE))).reshape(-1, 1)
    pos3 = positions.reshape(B, 1, S)

    ffn1_w = (ffn1_w_gate, ffn1_w_up, ffn1_w_down)
    ffn2_w = (ffn2_w_gate, ffn2_w_up, ffn2_w_down)
    w1 = tuple(w[0].astype(BF16) for w in ffn1_w)
    x = x.reshape(T, D)
    for i in range(depth):
        x, w2 = _ffn(x, ffn1_norm[i], *w1, tm_ffn, next_w=(ffn2_w, i))
        j = i // 2
        after = (ffn1_w, i + 1) if i + 1 < depth else None
        if i % 2 == 0:
            x, w1 = _pool_ffn(x, S, mix_norm[i], pool_w[j].astype(BF16), pool_scale[j], ffn2_norm[i], *w2,
                              tm_pool, next_w=after)
        else:
            bound = (Q_SCALE * QK_HEAD * jnp.max(jnp.abs(mla_q_head_norm[j]))
                     * jnp.max(jnp.abs(mla_k_head_norm[j]))).reshape(1)
            safe = (2.0 * bound <= F32_SAFE_EXP2).astype(jnp.int32)
            qT, k, vT = _mla_pre(bound, x.reshape(B, S, D), pos3, invf, mix_norm[i], mla_w_in[j].T.astype(BF16),
                                 mla_q_norm[j], mla_w_q_up[j].T.astype(BF16), mla_kv_norm[j],
                                 mla_w_kv_up[j].T.astype(BF16), mla_q_head_norm[j], mla_k_head_norm[j], n_heads, tk)
            o = _attn(safe, qT, k, vT)
            x, w1 = _ffn(x, ffn2_norm[i], *w2, tm_ffn, attn_out=o.reshape(T, n_heads * V_HEAD),
                         w_out=mla_w_out[j].astype(BF16), next_w=after)
    return x.reshape(B, S, D)
```

```python
import functools
import itertools
import math

import jax
import jax.numpy as jnp
from jax import lax
from jax.experimental import pallas as pl
from jax.experimental.pallas import tpu as pltpu

F32 = jnp.float32
BF16 = jnp.bfloat16

EPS = 1e-6
FFN_HALF = 0.5
FFN_CHUNK = 256
ATTN_TILE = 1024
MLA_SUB_TILES = 2
POOL_FFN_CHUNK = 256
POOL_WINDOWS = (2, 4, 8, 16)
POOL_HALO = 16
QK_NOPE = 128
QK_ROPE = 64
QK_HEAD = QK_NOPE + QK_ROPE
QK_PAD = 256
V_HEAD = 128
ROPE_THETA = 10000.0
LOG2E = math.log2(math.e)
Q_SCALE = QK_HEAD ** -0.5 * LOG2E
SUBLANES = 8
BF16_ROWS = 16
F32_SAFE_EXP2 = 120.0

VMEM_LIMIT = 56 * 1024 * 1024


def _cparams(sem):
    return pltpu.CompilerParams(dimension_semantics=sem, vmem_limit_bytes=VMEM_LIMIT)


def _resident(shape):
    nd = len(shape)
    return pl.BlockSpec(shape, lambda *_: (0,) * nd, pipeline_mode=pl.Buffered(1))


def _rms_rows(x, gain):
    ms = jnp.mean(x * x, axis=-1, keepdims=True)
    return x * lax.rsqrt(ms + EPS) * gain


def _swiglu_residual(x, g_ref, wg_ref, wu_ref, wd_ref, chunk=FFN_CHUNK, side_work=None):
    hn = _rms_rows(x, g_ref[...]).astype(BF16)
    d_ff = wg_ref.shape[1]
    y = None
    for c0 in range(0, d_ff, chunk):
        c1 = min(c0 + chunk, d_ff)
        g = jnp.dot(hn, wg_ref[:, c0:c1], preferred_element_type=F32)
        u = jnp.dot(hn, wu_ref[:, c0:c1], preferred_element_type=F32)
        a = (g / (1.0 + jnp.exp(-g)) * u).astype(BF16)
        yc = jnp.dot(a, wd_ref[c0:c1, :], preferred_element_type=F32)
        y = yc if y is None else y + yc
        if side_work is not None:
            next(side_work, None)
    if side_work is not None:
        for _ in side_work:
            pass
    return x + FFN_HALF * y


def _cast_pieces(src_refs, dst_refs):
    for src, dst in zip(src_refs, dst_refs):
        dst[...] = src[...].astype(BF16)
        yield


def _cast_plan(next_w, n_steps):
    if next_w is None:
        return [], [], [], ()
    stacked, layer = next_w
    in_specs, out_specs, out_shapes = [], [], []
    for w in stacked:
        _, rows, cols = w.shape
        rb = next(r for r in range(BF16_ROWS, rows + 1, BF16_ROWS) if rows % r == 0 and rows // r <= n_steps)
        last = rows // rb - 1
        in_specs.append(pl.BlockSpec((None, rb, cols), lambda s, last=last: (layer, jnp.minimum(s, last), 0)))
        out_specs.append(pl.BlockSpec((rb, cols), lambda s, last=last: (jnp.minimum(s, last), 0)))
        out_shapes.append(jax.ShapeDtypeStruct((rows, cols), BF16))
    return in_specs, out_specs, out_shapes, tuple(stacked)


def _ffn_kernel(*refs, n_cast, with_proj):
    n_in = (7 if with_proj else 5) + n_cast
    ins, o_ref, cast_out = refs[:n_in - n_cast], refs[n_in], refs[n_in + 1:]
    cast_in = refs[n_in - n_cast:n_in]
    if with_proj:
        x_ref, a_ref, wo_ref, g_ref, wg_ref, wu_ref, wd_ref = ins
        x = x_ref[...] + jnp.dot(a_ref[...], wo_ref[...], preferred_element_type=F32)
    else:
        x_ref, g_ref, wg_ref, wu_ref, wd_ref = ins
        x = x_ref[...]
    o_ref[...] = _swiglu_residual(x, g_ref, wg_ref, wu_ref, wd_ref, side_work=_cast_pieces(cast_in, cast_out))


def _ffn(x2, gain, wg, wu, wd, tm, attn_out=None, w_out=None, next_w=None):
    T, D = x2.shape
    F = wg.shape[1]
    n_steps = T // tm
    row_tile = pl.BlockSpec((tm, D), lambda i: (i, 0))
    specs = [row_tile]
    args = (x2,)
    if attn_out is not None:
        specs += [pl.BlockSpec((tm, attn_out.shape[1]), lambda i: (i, 0)), _resident(w_out.shape)]
        args += (attn_out, w_out)
    specs += [_resident((1, D)), _resident((D, F)), _resident((D, F)), _resident((F, D))]
    args += (gain.reshape(1, D), wg, wu, wd)
    cast_in, cast_out, cast_shapes, cast_args = _cast_plan(next_w, n_steps)
    out = pl.pallas_call(
        functools.partial(_ffn_kernel, n_cast=len(cast_args), with_proj=attn_out is not None),
        grid=(n_steps,),
        in_specs=specs + cast_in,
        out_specs=[row_tile] + cast_out,
        out_shape=[jax.ShapeDtypeStruct((T, D), F32)] + cast_shapes,
        compiler_params=_cparams(("arbitrary",)),
        name="ffn" if attn_out is None else "proj_ffn",
    )(*args, *cast_args)
    return out[0], tuple(out[1:])


def _pool_mix(x, halo, first_of_seq, row0, g_ref, w_ref, sc_ref, hbuf, dst_ref):
    tm = x.shape[0]
    gain = g_ref[...]
    hn = _rms_rows(x, gain)
    hh = _rms_rows(halo, gain)
    hbuf[0:POOL_HALO] = jnp.where(first_of_seq, 0.0, hh)
    hbuf[POOL_HALO:POOL_HALO + tm] = hn
    yield
    C = w_ref.shape[1]
    rows = tm // 2
    for g, w_len in enumerate(POOL_WINDOWS):
        cols = slice(g * C, (g + 1) * C)
        for r0 in (0, rows):
            u = hbuf[POOL_HALO + r0:POOL_HALO + r0 + rows, cols]
            wsum = u
            for j in range(1, w_len):
                wsum = wsum + hbuf[POOL_HALO + r0 - j:POOL_HALO + r0 - j + rows, cols]
            t = row0 + r0 + lax.broadcasted_iota(jnp.int32, (rows, 1), 0)
            count = jnp.minimum(t + 1, w_len).astype(F32)
            pooled = (wsum / count - u).astype(BF16)
            y = jnp.dot(pooled, w_ref[g], preferred_element_type=F32)
            dst_ref[r0:r0 + rows, cols] = x[r0:r0 + rows, cols] + y * sc_ref[:, cols]
            yield


def _pool_ffn_kernel(*refs, n_tiles, tiles_per_seq, n_cast):
    x_ref, halo_ref, gmix_ref, pw_ref, psc_ref, g_ref, wg_ref, wu_ref, wd_ref = refs[:9]
    cast_in, o_ref, cast_out = refs[9:9 + n_cast], refs[9 + n_cast], refs[10 + n_cast:10 + 2 * n_cast]
    hbuf, xm0_ref, xm1_ref = refs[10 + 2 * n_cast:]
    s = pl.program_id(0)
    tm = x_ref.shape[0]

    def side(dst_ref=None):
        casts = _cast_pieces(cast_in, cast_out)
        if dst_ref is None:
            return casts
        i = s % tiles_per_seq
        mix = _pool_mix(x_ref[...], halo_ref[...], i == 0, i * tm, gmix_ref, pw_ref, psc_ref, hbuf, dst_ref)
        return itertools.chain(mix, casts)

    def ffn_from(src_ref, side_work):
        o_ref[...] = _swiglu_residual(src_ref[...], g_ref, wg_ref, wu_ref, wd_ref, chunk=POOL_FFN_CHUNK,
                                      side_work=side_work)

    last_ref = xm0_ref if (n_tiles - 1) % 2 == 0 else xm1_ref
    middle = jnp.logical_and(s > 0, s < n_tiles)

    @pl.when(s == 0)
    def _():
        for _ in side(xm0_ref):
            pass

    @pl.when(jnp.logical_and(middle, s % 2 == 1))
    def _():
        ffn_from(xm0_ref, side(xm1_ref))

    @pl.when(jnp.logical_and(middle, s % 2 == 0))
    def _():
        ffn_from(xm1_ref, side(xm0_ref))

    @pl.when(s == n_tiles)
    def _():
        ffn_from(last_ref, side())


def _pool_ffn(x2, seq_len, gmix, pool_w, pool_scale, gain, wg, wu, wd, tm, next_w=None):
    T, D = x2.shape
    F = wg.shape[1]
    G, C, _ = pool_w.shape
    n_tiles = T // tm
    hb = tm // POOL_HALO
    cast_in, cast_out, cast_shapes, cast_args = _cast_plan(next_w, n_tiles + 1)
    kern = functools.partial(_pool_ffn_kernel, n_tiles=n_tiles, tiles_per_seq=seq_len // tm, n_cast=len(cast_args))
    out = pl.pallas_call(
        kern,
        grid=(n_tiles + 1,),
        in_specs=[
            pl.BlockSpec((tm, D), lambda s: (jnp.minimum(s, n_tiles - 1), 0)),
            pl.BlockSpec((POOL_HALO, D), lambda s: (jnp.maximum(jnp.minimum(s, n_tiles - 1) * hb - 1, 0), 0)),
            _resident((1, D)),
            _resident((G, C, C)),
            _resident((1, D)),
            _resident((1, D)),
            _resident((D, F)),
            _resident((D, F)),
            _resident((F, D)),
        ] + cast_in,
        out_specs=[pl.BlockSpec((tm, D), lambda s: (jnp.maximum(s - 1, 0), 0))] + cast_out,
        out_shape=[jax.ShapeDtypeStruct((T, D), F32)] + cast_shapes,
        scratch_shapes=[pltpu.VMEM((POOL_HALO + tm, D), F32), pltpu.VMEM((tm, D), F32), pltpu.VMEM((tm, D), F32)],
        compiler_params=_cparams(("arbitrary",)),
        name="pool_ffn",
    )(x2, x2, gmix.reshape(1, D), pool_w, pool_scale.reshape(1, D), gain.reshape(1, D), wg, wu, wd, *cast_args)
    return out[0], tuple(out[1:])


def _rms_cols(xT, gain_col, n):
    ms = jnp.sum(xT * xT, axis=0, keepdims=True) * (1.0 / n)
    return xT * lax.rsqrt(ms + EPS) * gain_col


def _mla_pre_kernel(bound_ref, x_ref, pos_ref, invf_ref, gmix_ref, winT_ref, gq_ref, wqT_ref, gkv_ref, wkvT_ref,
                    gqh_ref, gkh_ref, qT_ref, k_ref, vT_ref, *, n_heads, q_lora, kv_lora, q_scale):
    tm = x_ref.shape[1]
    half = QK_ROPE // 2
    kv_head = QK_NOPE + V_HEAD

    def sub_tile(t0, tn):
        tok = slice(t0, t0 + tn)
        hn = _rms_rows(x_ref[0, tok, :], gmix_ref[...]).astype(BF16)
        latT = lax.dot_general(winT_ref[...], hn, (((1,), (1,)), ((), ())), preferred_element_type=F32)
        yield
        kpeT = latT[q_lora + kv_lora:]
        cqn = _rms_cols(latT[:q_lora], gq_ref[...], q_lora).astype(BF16)
        ckvn = _rms_cols(latT[q_lora:q_lora + kv_lora], gkv_ref[...], kv_lora).astype(BF16)
        ang = pos_ref[0, :, tok].astype(F32) * invf_ref[...]
        cos, sin = jnp.cos(ang), jnp.sin(ang)
        gqh = gqh_ref[...]
        gkh = gkh_ref[...]
        kpe_ss = jnp.sum(kpeT * kpeT, axis=0, keepdims=True)
        first = lax.broadcasted_iota(jnp.int32, (QK_PAD - QK_HEAD, tn), 0) == 0
        q_pad = jnp.where(first, 1.0, 0.0)
        k_pad = jnp.where(first, -bound_ref[0], 0.0)

        def rope(x1, x2):
            return x1 * cos - x2 * sin, x2 * cos + x1 * sin

        yield
        for h in range(n_heads):
            qh = jnp.dot(wqT_ref[h * QK_HEAD:(h + 1) * QK_HEAD, :], cqn, preferred_element_type=F32)
            kvh = jnp.dot(wkvT_ref[h * kv_head:(h + 1) * kv_head, :], ckvn, preferred_element_type=F32)
            ms = jnp.sum(qh * qh, axis=0, keepdims=True) * (1.0 / QK_HEAD)
            qn = qh * (lax.rsqrt(ms + EPS) * q_scale) * gqh
            r1, r2 = rope(qn[QK_NOPE:QK_NOPE + half], qn[QK_NOPE + half:])
            qT_ref[0, h, :, tok] = jnp.concatenate([qn[:QK_NOPE], r1, r2, q_pad], axis=0).astype(BF16)

            kn = kvh[:QK_NOPE]
            ms = (jnp.sum(kn * kn, axis=0, keepdims=True) + kpe_ss) * (1.0 / QK_HEAD)
            rs = lax.rsqrt(ms + EPS)
            r1, r2 = rope(kpeT[:half] * rs * gkh[QK_NOPE:QK_NOPE + half],
                          kpeT[half:] * rs * gkh[QK_NOPE + half:])
            k_ref[0, h, 0, tok, :QK_NOPE] = (kn * rs * gkh[:QK_NOPE]).T.astype(BF16)
            k_ref[0, h, 0, tok, QK_NOPE:] = jnp.concatenate([r1, r2, k_pad], axis=0).T.astype(BF16)
            vT_ref[0, h, 0, :, tok] = kvh[QK_NOPE:].astype(BF16)
            yield

    n_sub = MLA_SUB_TILES if tm % (MLA_SUB_TILES * 128) == 0 else 1
    stages = [sub_tile(i * (tm // n_sub), tm // n_sub) for i in range(n_sub)]
    live = list(stages)
    lead = 0
    while live:
        lead += 1
        for g in list(live[:lead]):
            if next(g, StopIteration) is StopIteration:
                live.remove(g)
                lead -= 1


def _mla_pre(bound, x3, pos3, invf, gmix, winT, gq, wqT, gkv, wkvT, gqh, gkh, n_heads, tm):
    B, S, D = x3.shape
    q_lora, kv_lora = wqT.shape[1], wkvT.shape[1]
    n_t = S // tm
    kern = functools.partial(_mla_pre_kernel, n_heads=n_heads, q_lora=q_lora, kv_lora=kv_lora,
                             q_scale=Q_SCALE)
    return pl.pallas_call(
        kern,
        grid=(B, n_t),
        in_specs=[
            pl.BlockSpec(memory_space=pltpu.SMEM),
            pl.BlockSpec((1, tm, D), lambda b, i: (b, i, 0)),
            pl.BlockSpec((1, 1, tm), lambda b, i: (b, 0, i)),
            _resident(invf.shape),
            _resident((1, D)),
            _resident(winT.shape),
            _resident((q_lora, 1)),
            _resident(wqT.shape),
            _resident((kv_lora, 1)),
            _resident(wkvT.shape),
            _resident((QK_HEAD, 1)),
            _resident((QK_HEAD, 1)),
        ],
        out_specs=[
            pl.BlockSpec((1, n_heads, QK_PAD, tm), lambda b, i: (b, 0, 0, i)),
            pl.BlockSpec((1, n_heads, 1, tm, QK_PAD), lambda b, i: (b, 0, i, 0, 0)),
            pl.BlockSpec((1, n_heads, 1, V_HEAD, tm), lambda b, i: (b, 0, i, 0, 0)),
        ],
        out_shape=[
            jax.ShapeDtypeStruct((B, n_heads, QK_PAD, S), BF16),
            jax.ShapeDtypeStruct((B, n_heads, n_t, tm, QK_PAD), BF16),
            jax.ShapeDtypeStruct((B, n_heads, n_t, V_HEAD, tm), BF16),
        ],
        compiler_params=_cparams(("parallel", "parallel")),
        name="mla_pre",
    )(bound, x3, pos3, invf, gmix.reshape(1, D), winT, gq.reshape(-1, 1), wqT, gkv.reshape(-1, 1), wkvT,
      gqh.reshape(-1, 1), gkh.reshape(-1, 1))


def _attn_kernel(safe_ref, qT_ref, k_ref, vT_ref, o_ref, m_ref, l_ref, acc_ref, p0_ref, p1_ref):
    i = pl.program_id(2)
    sub = l_ref.shape[0]
    tk, tq = p0_ref.shape
    v_w = vT_ref.shape[-1]
    l_ref[...] = jnp.zeros(l_ref.shape, F32)
    acc_ref[...] = jnp.zeros(acc_ref.shape, F32)

    def scores(j, diagonal):
        sT = jnp.dot(k_ref[0, 0, j], qT_ref[0, 0], preferred_element_type=F32)
        if not diagonal:
            return sT
        key = lax.broadcasted_iota(jnp.int32, sT.shape, 0)
        qry = lax.broadcasted_iota(jnp.int32, sT.shape, 1)
        return jnp.where(key <= qry, sT, -jnp.inf)

    def partial_sum(p):
        return jnp.sum(p.reshape(p.shape[0] // sub, sub, p.shape[1]), axis=0)

    def probs(j, p_ref, diagonal=False):
        p = jnp.exp2(scores(j, diagonal))
        l_ref[...] += partial_sum(p)
        p_ref[...] = p.astype(BF16)

    def pv_value(j, p):
        out = None
        for c in range(tk // v_w):
            part = jnp.dot(vT_ref[0, 0, j * (tk // v_w) + c], p[c * v_w:(c + 1) * v_w, :],
                           preferred_element_type=F32)
            out = part if out is None else out + part
        return out

    def pv(j, p_ref):
        acc_ref[...] += pv_value(j, p_ref)

    def online(j, diagonal=False):
        sT = scores(j, diagonal)
        m_old = m_ref[...]
        m_new = jnp.maximum(m_old, jnp.max(sT, axis=0, keepdims=True))
        alpha = jnp.exp2(m_old - m_new)
        p = jnp.exp2(sT - m_new)
        l_ref[...] = alpha * l_ref[...] + partial_sum(p)
        acc_ref[...] = alpha * acc_ref[...] + pv_value(j, p.astype(BF16))
        m_ref[...] = m_new

    @pl.when(safe_ref[0] != 0)
    def _():
        probs(i, p0_ref, diagonal=True)

        def pair(t, carry):
            probs(2 * t, p1_ref)
            pv(jnp.where(t == 0, i, 2 * t - 1), p0_ref)
            probs(2 * t + 1, p0_ref)
            pv(2 * t, p1_ref)
            return carry

        lax.fori_loop(0, i // 2, pair, 0)

        @pl.when(i % 2 == 0)
        def _():
            pv(jnp.maximum(i - 1, 0), p0_ref)

        @pl.when(i % 2 == 1)
        def _():
            probs(i - 1, p1_ref)
            pv(jnp.where(i == 1, i, i - 2), p0_ref)
            pv(i - 1, p1_ref)

    @pl.when(safe_ref[0] == 0)
    def _():
        m_ref[...] = jnp.full(m_ref.shape, -jnp.inf, F32)

        def body(j, carry):
            online(j)
            return carry

        lax.fori_loop(0, i, body, 0)
        online(i, diagonal=True)

    l = jnp.sum(l_ref[...], axis=0, keepdims=True)
    o_ref[0] = (acc_ref[...] / l).T.astype(o_ref.dtype)


def _attn(safe, qT, k, vT):
    B, H, _, S = qT.shape
    n_v, v_w = vT.shape[2], vT.shape[4]
    tq = ATTN_TILE if S % ATTN_TILE == 0 else v_w
    assert S % tq == 0 and tq % v_w == 0, (S, tq, v_w)
    k = k.reshape(B, H, S // tq, tq, QK_PAD)
    return pl.pallas_call(
        _attn_kernel,
        grid=(B, H, S // tq),
        in_specs=[
            pl.BlockSpec(memory_space=pltpu.SMEM),
            pl.BlockSpec((1, 1, QK_PAD, tq), lambda b, h, i: (b, h, 0, i)),
            pl.BlockSpec((1, 1, S // tq, tq, QK_PAD), lambda b, h, i: (b, h, 0, 0, 0)),
            pl.BlockSpec((1, 1, n_v, V_HEAD, v_w), lambda b, h, i: (b, h, 0, 0, 0)),
        ],
        out_specs=pl.BlockSpec((1, tq, V_HEAD), lambda b, h, i: (b, i, h)),
        out_shape=jax.ShapeDtypeStruct((B, S, H * V_HEAD), BF16),
        scratch_shapes=[pltpu.VMEM((1, tq), F32), pltpu.VMEM((SUBLANES, tq), F32),
                        pltpu.VMEM((V_HEAD, tq), F32), pltpu.VMEM((tq, tq), BF16), pltpu.VMEM((tq, tq), BF16)],
        compiler_params=_cparams(("parallel", "parallel", "arbitrary")),
        name="attn",
    )(safe, qT, k, vT)


def _tile(n, want):
    t = min(n, want)
    assert n % t == 0, (n, t)
    return t


def kernel(x, positions, ffn1_norm, ffn1_w_gate, ffn1_w_up, ffn1_w_down, mix_norm, pool_w, pool_scale,
           mla_w_in, mla_q_norm, mla_w_q_up, mla_kv_norm, mla_w_kv_up, mla_q_head_norm, mla_k_head_norm,
           mla_w_out, ffn2_norm, ffn2_w_gate, ffn2_w_up, ffn2_w_down):
    B, S, D = x.shape
    depth = ffn1_norm.shape[0]
    n_heads = mla_w_out.shape[1] // V_HEAD
    T = B * S
    tm_ffn = _tile(T, 1024)
    tm_pool = _tile(S, 512)
    tk = _tile(S, 512)

    invf = (1.0 / (ROPE_THETA ** (jnp.arange(0, QK_ROPE, 2, dtype=F32) / QK_ROPE))).reshape(-1, 1)
    pos3 = positions.reshape(B, 1, S)

    ffn1_w = (ffn1_w_gate, ffn1_w_up, ffn1_w_down)
    ffn2_w = (ffn2_w_gate, ffn2_w_up, ffn2_w_down)
    w1 = tuple(w[0].astype(BF16) for w in ffn1_w)
    x = x.reshape(T, D)
    for i in range(depth):
        x, w2 = _ffn(x, ffn1_norm[i], *w1, tm_ffn, next_w=(ffn2_w, i))
        j = i // 2
        after = (ffn1_w, i + 1) if i + 1 < depth else None
        if i % 2 == 0:
            x, w1 = _pool_ffn(x, S, mix_norm[i], pool_w[j].astype(BF16), pool_scale[j], ffn2_norm[i], *w2,
                              tm_pool, next_w=after)
        else:
            bound = (Q_SCALE * QK_HEAD * jnp.max(jnp.abs(mla_q_head_norm[j]))
                     * jnp.max(jnp.abs(mla_k_head_norm[j]))).reshape(1)
            safe = (2.0 * bound <= F32_SAFE_EXP2).astype(jnp.int32)
            qT, k, vT = _mla_pre(bound, x.reshape(B, S, D), pos3, invf, mix_norm[i], mla_w_in[j].T.astype(BF16),
                                 mla_q_norm[j], mla_w_q_up[j].T.astype(BF16), mla_kv_norm[j],
                                 mla_w_kv_up[j].T.astype(BF16), mla_q_head_norm[j], mla_k_head_norm[j], n_heads, tk)
            o = _attn(safe, qT, k, vT)
            x, w1 = _ffn(x, ffn2_norm[i], *w2, tm_ffn, attn_out=o.reshape(T, n_heads * V_HEAD),
                         w_out=mla_w_out[j].astype(BF16), next_w=after)
    return x.reshape(B, S, D)
```

```python
import functools
import itertools
import math

import jax
import jax.numpy as jnp
from jax import lax
from jax.experimental import pallas as pl
from jax.experimental.pallas import tpu as pltpu

F32 = jnp.float32
BF16 = jnp.bfloat16

EPS = 1e-6
FFN_HALF = 0.5
FFN_CHUNK = 256
ATTN_TILE = 1024
MLA_SUB_TILES = 2
POOL_FFN_CHUNK = 256
POOL_WINDOWS = (2, 4, 8, 16)
POOL_HALO = 16
QK_NOPE = 128
QK_ROPE = 64
QK_HEAD = QK_NOPE + QK_ROPE
QK_PAD = 256
V_HEAD = 128
ROPE_THETA = 10000.0
LOG2E = math.log2(math.e)
Q_SCALE = QK_HEAD ** -0.5 * LOG2E
SUBLANES = 8
BF16_ROWS = 16
F32_SAFE_EXP2 = 120.0

VMEM_LIMIT = 56 * 1024 * 1024


def _cparams(sem):
    return pltpu.CompilerParams(dimension_semantics=sem, vmem_limit_bytes=VMEM_LIMIT)


def _resident(shape):
    nd = len(shape)
    return pl.BlockSpec(shape, lambda *_: (0,) * nd, pipeline_mode=pl.Buffered(1))


def _rms_rows(x, gain):
    ms = jnp.mean(x * x, axis=-1, keepdims=True)
    return x * lax.rsqrt(ms + EPS) * gain


def _swiglu_residual(x, g_ref, wg_ref, wu_ref, wd_ref, chunk=FFN_CHUNK, side_work=None):
    hn = _rms_rows(x, g_ref[...]).astype(BF16)
    d_ff = wg_ref.shape[1]
    y = None
    for c0 in range(0, d_ff, chunk):
        c1 = min(c0 + chunk, d_ff)
        g = jnp.dot(hn, wg_ref[:, c0:c1], preferred_element_type=F32)
        u = jnp.dot(hn, wu_ref[:, c0:c1], preferred_element_type=F32)
        a = (g / (1.0 + jnp.exp(-g)) * u).astype(BF16)
        yc = jnp.dot(a, wd_ref[c0:c1, :], preferred_element_type=F32)
        y = yc if y is None else y + yc
        if side_work is not None:
            next(side_work, None)
    if side_work is not None:
        for _ in side_work:
            pass
    return x + FFN_HALF * y


def _cast_pieces(src_refs, dst_refs):
    for src, dst in zip(src_refs, dst_refs):
        dst[...] = src[...].astype(BF16)
        yield


def _cast_plan(next_w, n_steps):
    if next_w is None:
        return [], [], [], ()
    stacked, layer = next_w
    in_specs, out_specs, out_shapes = [], [], []
    for w in stacked:
        _, rows, cols = w.shape
        rb = next(r for r in range(BF16_ROWS, rows + 1, BF16_ROWS) if rows % r == 0 and rows // r <= n_steps)
        last = rows // rb - 1
        in_specs.append(pl.BlockSpec((None, rb, cols), lambda s, last=last: (layer, jnp.minimum(s, last), 0)))
        out_specs.append(pl.BlockSpec((rb, cols), lambda s, last=last: (jnp.minimum(s, last), 0)))
        out_shapes.append(jax.ShapeDtypeStruct((rows, cols), BF16))
    return in_specs, out_specs, out_shapes, tuple(stacked)


def _ffn_kernel(*refs, n_cast, with_proj):
    n_in = (7 if with_proj else 5) + n_cast
    ins, o_ref, cast_out = refs[:n_in - n_cast], refs[n_in], refs[n_in + 1:]
    cast_in = refs[n_in - n_cast:n_in]
    if with_proj:
        x_ref, a_ref, wo_ref, g_ref, wg_ref, wu_ref, wd_ref = ins
        x = x_ref[...] + jnp.dot(a_ref[...], wo_ref[...], preferred_element_type=F32)
    else:
        x_ref, g_ref, wg_ref, wu_ref, wd_ref = ins
        x = x_ref[...]
    o_ref[...] = _swiglu_residual(x, g_ref, wg_ref, wu_ref, wd_ref, side_work=_cast_pieces(cast_in, cast_out))


def _ffn(x2, gain, wg, wu, wd, tm, attn_out=None, w_out=None, next_w=None):
    T, D = x2.shape
    F = wg.shape[1]
    n_steps = T // tm
    row_tile = pl.BlockSpec((tm, D), lambda i: (i, 0))
    specs = [row_tile]
    args = (x2,)
    if attn_out is not None:
        specs += [pl.BlockSpec((tm, attn_out.shape[1]), lambda i: (i, 0)), _resident(w_out.shape)]
        args += (attn_out, w_out)
    specs += [_resident((1, D)), _resident((D, F)), _resident((D, F)), _resident((F, D))]
    args += (gain.reshape(1, D), wg, wu, wd)
    cast_in, cast_out, cast_shapes, cast_args = _cast_plan(next_w, n_steps)
    out = pl.pallas_call(
        functools.partial(_ffn_kernel, n_cast=len(cast_args), with_proj=attn_out is not None),
        grid=(n_steps,),
        in_specs=specs + cast_in,
        out_specs=[row_tile] + cast_out,
        out_shape=[jax.ShapeDtypeStruct((T, D), F32)] + cast_shapes,
        compiler_params=_cparams(("arbitrary",)),
        name="ffn" if attn_out is None else "proj_ffn",
    )(*args, *cast_args)
    return out[0], tuple(out[1:])


def _pool_mix(x, halo, first_of_seq, row0, g_ref, w_ref, sc_ref, hbuf, dst_ref):
    tm = x.shape[0]
    gain = g_ref[...]
    hn = _rms_rows(x, gain)
    hh = _rms_rows(halo, gain)
    hbuf[0:POOL_HALO] = jnp.where(first_of_seq, 0.0, hh)
    hbuf[POOL_HALO:POOL_HALO + tm] = hn
    yield
    C = w_ref.shape[1]
    rows = tm // 2
    for g, w_len in enumerate(POOL_WINDOWS):
        cols = slice(g * C, (g + 1) * C)
        for r0 in (0, rows):
            u = hbuf[POOL_HALO + r0:POOL_HALO + r0 + rows, cols]
            wsum = u
            for j in range(1, w_len):
                wsum = wsum + hbuf[POOL_HALO + r0 - j:POOL_HALO + r0 - j + rows, cols]
            t = row0 + r0 + lax.broadcasted_iota(jnp.int32, (rows, 1), 0)
            count = jnp.minimum(t + 1, w_len).astype(F32)
            pooled = (wsum / count - u).astype(BF16)
            y = jnp.dot(pooled, w_ref[g], preferred_element_type=F32)
            dst_ref[r0:r0 + rows, cols] = x[r0:r0 + rows, cols] + y * sc_ref[:, cols]
            yield


def _pool_ffn_kernel(*refs, n_tiles, tiles_per_seq, n_cast):
    x_ref, halo_ref, gmix_ref, pw_ref, psc_ref, g_ref, wg_ref, wu_ref, wd_ref = refs[:9]
    cast_in, o_ref, cast_out = refs[9:9 + n_cast], refs[9 + n_cast], refs[10 + n_cast:10 + 2 * n_cast]
    hbuf, xm0_ref, xm1_ref = refs[10 + 2 * n_cast:]
    s = pl.program_id(0)
    tm = x_ref.shape[0]

    def side(dst_ref=None):
        casts = _cast_pieces(cast_in, cast_out)
        if dst_ref is None:
            return casts
        i = s % tiles_per_seq
        mix = _pool_mix(x_ref[...], halo_ref[...], i == 0, i * tm, gmix_ref, pw_ref, psc_ref, hbuf, dst_ref)
        return itertools.chain(mix, casts)

    def ffn_from(src_ref, side_work):
        o_ref[...] = _swiglu_residual(src_ref[...], g_ref, wg_ref, wu_ref, wd_ref, chunk=POOL_FFN_CHUNK,
                                      side_work=side_work)

    last_ref = xm0_ref if (n_tiles - 1) % 2 == 0 else xm1_ref
    middle = jnp.logical_and(s > 0, s < n_tiles)

    @pl.when(s == 0)
    def _():
        for _ in side(xm0_ref):
            pass

    @pl.when(jnp.logical_and(middle, s % 2 == 1))
    def _():
        ffn_from(xm0_ref, side(xm1_ref))

    @pl.when(jnp.logical_and(middle, s % 2 == 0))
    def _():
        ffn_from(xm1_ref, side(xm0_ref))

    @pl.when(s == n_tiles)
    def _():
        ffn_from(last_ref, side())


def _pool_ffn(x2, seq_len, gmix, pool_w, pool_scale, gain, wg, wu, wd, tm, next_w=None):
    T, D = x2.shape
    F = wg.shape[1]
    G, C, _ = pool_w.shape
    n_tiles = T // tm
    hb = tm // POOL_HALO
    cast_in, cast_out, cast_shapes, cast_args = _cast_plan(next_w, n_tiles + 1)
    kern = functools.partial(_pool_ffn_kernel, n_tiles=n_tiles, tiles_per_seq=seq_len // tm, n_cast=len(cast_args))
    out = pl.pallas_call(
        kern,
        grid=(n_tiles + 1,),
        in_specs=[
            pl.BlockSpec((tm, D), lambda s: (jnp.minimum(s, n_tiles - 1), 0)),
            pl.BlockSpec((POOL_HALO, D), lambda s: (jnp.maximum(jnp.minimum(s, n_tiles - 1) * hb - 1, 0), 0)),
            _resident((1, D)),
            _resident((G, C, C)),
            _resident((1, D)),
            _resident((1, D)),
            _resident((D, F)),
            _resident((D, F)),
            _resident((F, D)),
        ] + cast_in,
        out_specs=[pl.BlockSpec((tm, D), lambda s: (jnp.maximum(s - 1, 0), 0))] + cast_out,
        out_shape=[jax.ShapeDtypeStruct((T, D), F32)] + cast_shapes,
        scratch_shapes=[pltpu.VMEM((POOL_HALO + tm, D), F32), pltpu.VMEM((tm, D), F32), pltpu.VMEM((tm, D), F32)],
        compiler_params=_cparams(("arbitrary",)),
        name="pool_ffn",
    )(x2, x2, gmix.reshape(1, D), pool_w, pool_scale.reshape(1, D), gain.reshape(1, D), wg, wu, wd, *cast_args)
    return out[0], tuple(out[1:])


def _rms_cols(xT, gain_col, n):
    ms = jnp.sum(xT * xT, axis=0, keepdims=True) * (1.0 / n)
    return xT * lax.rsqrt(ms + EPS) * gain_col


def _mla_pre_kernel(bound_ref, x_ref, pos_ref, invf_ref, gmix_ref, winT_ref, gq_ref, wqT_ref, gkv_ref, wkvT_ref,
                    gqh_ref, gkh_ref, qT_ref, k_ref, vT_ref, *, n_heads, q_lora, kv_lora, q_scale):
    tm = x_ref.shape[1]
    half = QK_ROPE // 2
    kv_head = QK_NOPE + V_HEAD

    def sub_tile(t0, tn):
        tok = slice(t0, t0 + tn)
        hn = _rms_rows(x_ref[0, tok, :], gmix_ref[...]).astype(BF16)
        latT = lax.dot_general(winT_ref[...], hn, (((1,), (1,)), ((), ())), preferred_element_type=F32)
        yield
        kpeT = latT[q_lora + kv_lora:]
        cqn = _rms_cols(latT[:q_lora], gq_ref[...], q_lora).astype(BF16)
        ckvn = _rms_cols(latT[q_lora:q_lora + kv_lora], gkv_ref[...], kv_lora).astype(BF16)
        ang = pos_ref[0, :, tok].astype(F32) * invf_ref[...]
        cos, sin = jnp.cos(ang), jnp.sin(ang)
        gqh = gqh_ref[...]
        gkh = gkh_ref[...]
        kpe_ss = jnp.sum(kpeT * kpeT, axis=0, keepdims=True)
        first = lax.broadcasted_iota(jnp.int32, (QK_PAD - QK_HEAD, tn), 0) == 0
        q_pad = jnp.where(first, 1.0, 0.0)
        k_pad = jnp.where(first, -bound_ref[0], 0.0)

        def rope(x1, x2):
            return x1 * cos - x2 * sin, x2 * cos + x1 * sin

        yield
        for h in range(n_heads):
            qh = jnp.dot(wqT_ref[h * QK_HEAD:(h + 1) * QK_HEAD, :], cqn, preferred_element_type=F32)
            kvh = jnp.dot(wkvT_ref[h * kv_head:(h + 1) * kv_head, :], ckvn, preferred_element_type=F32)
            ms = jnp.sum(qh * qh, axis=0, keepdims=True) * (1.0 / QK_HEAD)
            qn = qh * (lax.rsqrt(ms + EPS) * q_scale) * gqh
            r1, r2 = rope(qn[QK_NOPE:QK_NOPE + half], qn[QK_NOPE + half:])
            qT_ref[0, h, :, tok] = jnp.concatenate([qn[:QK_NOPE], r1, r2, q_pad], axis=0).astype(BF16)

            kn = kvh[:QK_NOPE]
            ms = (jnp.sum(kn * kn, axis=0, keepdims=True) + kpe_ss) * (1.0 / QK_HEAD)
            rs = lax.rsqrt(ms + EPS)
            r1, r2 = rope(kpeT[:half] * rs * gkh[QK_NOPE:QK_NOPE + half],
                          kpeT[half:] * rs * gkh[QK_NOPE + half:])
            k_ref[0, h, 0, tok, :QK_NOPE] = (kn * rs * gkh[:QK_NOPE]).T.astype(BF16)
            k_ref[0, h, 0, tok, QK_NOPE:] = jnp.concatenate([r1, r2, k_pad], axis=0).T.astype(BF16)
            vT_ref[0, h, 0, :, tok] = kvh[QK_NOPE:].astype(BF16)
            yield

    n_sub = MLA_SUB_TILES if tm % (MLA_SUB_TILES * 128) == 0 else 1
    stages = [sub_tile(i * (tm // n_sub), tm // n_sub) for i in range(n_sub)]
    live = list(stages)
    lead = 0
    while live:
        lead += 1
        for g in list(live[:lead]):
            if next(g, StopIteration) is StopIteration:
                live.remove(g)
                lead -= 1


def _mla_pre(bound, x3, pos3, invf, gmix, winT, gq, wqT, gkv, wkvT, gqh, gkh, n_heads, tm):
    B, S, D = x3.shape
    q_lora, kv_lora = wqT.shape[1], wkvT.shape[1]
    n_t = S // tm
    kern = functools.partial(_mla_pre_kernel, n_heads=n_heads, q_lora=q_lora, kv_lora=kv_lora,
                             q_scale=Q_SCALE)
    return pl.pallas_call(
        kern,
        grid=(B, n_t),
        in_specs=[
            pl.BlockSpec(memory_space=pltpu.SMEM),
            pl.BlockSpec((1, tm, D), lambda b, i: (b, i, 0)),
            pl.BlockSpec((1, 1, tm), lambda b, i: (b, 0, i)),
            _resident(invf.shape),
            _resident((1, D)),
            _resident(winT.shape),
            _resident((q_lora, 1)),
            _resident(wqT.shape),
            _resident((kv_lora, 1)),
            _resident(wkvT.shape),
            _resident((QK_HEAD, 1)),
            _resident((QK_HEAD, 1)),
        ],
        out_specs=[
            pl.BlockSpec((1, n_heads, QK_PAD, tm), lambda b, i: (b, 0, 0, i)),
            pl.BlockSpec((1, n_heads, 1, tm, QK_PAD), lambda b, i: (b, 0, i, 0, 0)),
            pl.BlockSpec((1, n_heads, 1, V_HEAD, tm), lambda b, i: (b, 0, i, 0, 0)),
        ],
        out_shape=[
            jax.ShapeDtypeStruct((B, n_heads, QK_PAD, S), BF16),
            jax.ShapeDtypeStruct((B, n_heads, n_t, tm, QK_PAD), BF16),
            jax.ShapeDtypeStruct((B, n_heads, n_t, V_HEAD, tm), BF16),
        ],
        compiler_params=_cparams(("parallel", "parallel")),
        name="mla_pre",
    )(bound, x3, pos3, invf, gmix.reshape(1, D), winT, gq.reshape(-1, 1), wqT, gkv.reshape(-1, 1), wkvT,
      gqh.reshape(-1, 1), gkh.reshape(-1, 1))


def _attn_kernel(safe_ref, qT_ref, k_ref, vT_ref, o_ref, m_ref, l_ref, acc_ref, p0_ref, p1_ref):
    i = pl.program_id(2)
    sub = l_ref.shape[0]
    tk, tq = p0_ref.shape
    v_w = vT_ref.shape[-1]
    l_ref[...] = jnp.zeros(l_ref.shape, F32)
    acc_ref[...] = jnp.zeros(acc_ref.shape, F32)

    def scores(j, diagonal):
        sT = jnp.dot(k_ref[0, 0, j], qT_ref[0, 0], preferred_element_type=F32)
        if not diagonal:
            return sT
        key = lax.broadcasted_iota(jnp.int32, sT.shape, 0)
        qry = lax.broadcasted_iota(jnp.int32, sT.shape, 1)
        return jnp.where(key <= qry, sT, -jnp.inf)

    def partial_sum(p):
        return jnp.sum(p.reshape(p.shape[0] // sub, sub, p.shape[1]), axis=0)

    def probs(j, p_ref, diagonal=False):
        p = jnp.exp2(scores(j, diagonal))
        l_ref[...] += partial_sum(p)
        p_ref[...] = p.astype(BF16)

    def pv_value(j, p):
        out = None
        for c in range(tk // v_w):
            part = jnp.dot(vT_ref[0, 0, j * (tk // v_w) + c], p[c * v_w:(c + 1) * v_w, :],
                           preferred_element_type=F32)
            out = part if out is None else out + part
        return out

    def pv(j, p_ref):
        acc_ref[...] += pv_value(j, p_ref)

    def online(j, diagonal=False):
        sT = scores(j, diagonal)
        m_old = m_ref[...]
        m_new = jnp.maximum(m_old, jnp.max(sT, axis=0, keepdims=True))
        alpha = jnp.exp2(m_old - m_new)
        p = jnp.exp2(sT - m_new)
        l_ref[...] = alpha * l_ref[...] + partial_sum(p)
        acc_ref[...] = alpha * acc_ref[...] + pv_value(j, p.astype(BF16))
        m_ref[...] = m_new

    @pl.when(safe_ref[0] != 0)
    def _():
        probs(i, p0_ref, diagonal=True)

        def pair(t):
            probs(2 * t, p1_ref)
            pv(jnp.where(t == 0, i, 2 * t - 1), p0_ref)
            probs(2 * t + 1, p0_ref)
            pv(2 * t, p1_ref)

        def two_pairs(u, carry):
            pair(2 * u)
            pair(2 * u + 1)
            return carry

        n_pairs = i // 2
        lax.fori_loop(0, n_pairs // 2, two_pairs, 0)

        @pl.when(n_pairs % 2 == 1)
        def _():
            pair(n_pairs - 1)

        @pl.when(i % 2 == 0)
        def _():
            pv(jnp.maximum(i - 1, 0), p0_ref)

        @pl.when(i % 2 == 1)
        def _():
            probs(i - 1, p1_ref)
            pv(jnp.where(i == 1, i, i - 2), p0_ref)
            pv(i - 1, p1_ref)

    @pl.when(safe_ref[0] == 0)
    def _():
        m_ref[...] = jnp.full(m_ref.shape, -jnp.inf, F32)

        def body(j, carry):
            online(j)
            return carry

        lax.fori_loop(0, i, body, 0)
        online(i, diagonal=True)

    l = jnp.sum(l_ref[...], axis=0, keepdims=True)
    o_ref[0] = (acc_ref[...] / l).T.astype(o_ref.dtype)


def _attn(safe, qT, k, vT):
    B, H, _, S = qT.shape
    n_v, v_w = vT.shape[2], vT.shape[4]
    tq = ATTN_TILE if S % ATTN_TILE == 0 else v_w
    assert S % tq == 0 and tq % v_w == 0, (S, tq, v_w)
    k = k.reshape(B, H, S // tq, tq, QK_PAD)
    return pl.pallas_call(
        _attn_kernel,
        grid=(B, H, S // tq),
        in_specs=[
            pl.BlockSpec(memory_space=pltpu.SMEM),
            pl.BlockSpec((1, 1, QK_PAD, tq), lambda b, h, i: (b, h, 0, i)),
            pl.BlockSpec((1, 1, S // tq, tq, QK_PAD), lambda b, h, i: (b, h, 0, 0, 0)),
            pl.BlockSpec((1, 1, n_v, V_HEAD, v_w), lambda b, h, i: (b, h, 0, 0, 0)),
        ],
        out_specs=pl.BlockSpec((1, tq, V_HEAD), lambda b, h, i: (b, i, h)),
        out_shape=jax.ShapeDtypeStruct((B, S, H * V_HEAD), BF16),
        scratch_shapes=[pltpu.VMEM((1, tq), F32), pltpu.VMEM((SUBLANES, tq), F32),
                        pltpu.VMEM((V_HEAD, tq), F32), pltpu.VMEM((tq, tq), BF16), pltpu.VMEM((tq, tq), BF16)],
        compiler_params=_cparams(("parallel", "parallel", "arbitrary")),
        name="attn",
    )(safe, qT, k, vT)


def _tile(n, want):
    t = min(n, want)
    assert n % t == 0, (n, t)
    return t


def kernel(x, positions, ffn1_norm, ffn1_w_gate, ffn1_w_up, ffn1_w_down, mix_norm, pool_w, pool_scale,
           mla_w_in, mla_q_norm, mla_w_q_up, mla_kv_norm, mla_w_kv_up, mla_q_head_norm, mla_k_head_norm,
           mla_w_out, ffn2_norm, ffn2_w_gate, ffn2_w_up, ffn2_w_down):
    B, S, D = x.shape
    depth = ffn1_norm.shape[0]
    n_heads = mla_w_out.shape[1] // V_HEAD
    T = B * S
    tm_ffn = _tile(T, 1024)
    tm_pool = _tile(S, 512)
    tk = _tile(S, 512)

    invf = (1.0 / (ROPE_THETA ** (jnp.arange(0, QK_ROPE, 2, dtype=F32) / QK_ROPE))).reshape(-1, 1)
    pos3 = positions.reshape(B, 1, S)

    ffn1_w = (ffn1_w_gate, ffn1_w_up, ffn1_w_down)
    ffn2_w = (ffn2_w_gate, ffn2_w_up, ffn2_w_down)
    w1 = tuple(w[0].astype(BF16) for w in ffn1_w)
    x = x.reshape(T, D)
    for i in range(depth):
        x, w2 = _ffn(x, ffn1_norm[i], *w1, tm_ffn, next_w=(ffn2_w, i))
        j = i // 2
        after = (ffn1_w, i + 1) if i + 1 < depth else None
        if i % 2 == 0:
            x, w1 = _pool_ffn(x, S, mix_norm[i], pool_w[j].astype(BF16), pool_scale[j], ffn2_norm[i], *w2,
                              tm_pool, next_w=after)
        else:
            bound = (Q_SCALE * QK_HEAD * jnp.max(jnp.abs(mla_q_head_norm[j]))
                     * jnp.max(jnp.abs(mla_k_head_norm[j]))).reshape(1)
            safe = (2.0 * bound <= F32_SAFE_EXP2).astype(jnp.int32)
            qT, k, vT = _mla_pre(bound, x.reshape(B, S, D), pos3, invf, mix_norm[i], mla_w_in[j].T.astype(BF16),
                                 mla_q_norm[j], mla_w_q_up[j].T.astype(BF16), mla_kv_norm[j],
                                 mla_w_kv_up[j].T.astype(BF16), mla_q_head_norm[j], mla_k_head_norm[j], n_heads, tk)
            o = _attn(safe, qT, k, vT)
            x, w1 = _ffn(x, ffn2_norm[i], *w2, tm_ffn, attn_out=o.reshape(T, n_heads * V_HEAD),
                         w_out=mla_w_out[j].astype(BF16), next_w=after)
    return x.reshape(B, S, D)
```

```python
import functools
import itertools
import math

import jax
import jax.numpy as jnp
from jax import lax
from jax.experimental import pallas as pl
from jax.experimental.pallas import tpu as pltpu

F32 = jnp.float32
BF16 = jnp.bfloat16

EPS = 1e-6
FFN_HALF = 0.5
FFN_CHUNK = 256
ATTN_TILE = 1024
MLA_SUB_TILES = 4
POOL_FFN_CHUNK = 256
POOL_WINDOWS = (2, 4, 8, 16)
POOL_HALO = 16
QK_NOPE = 128
QK_ROPE = 64
QK_HEAD = QK_NOPE + QK_ROPE
QK_PAD = 256
V_HEAD = 128
ROPE_THETA = 10000.0
LOG2E = math.log2(math.e)
Q_SCALE = QK_HEAD ** -0.5 * LOG2E
SUBLANES = 8
BF16_ROWS = 16
F32_SAFE_EXP2 = 120.0

VMEM_LIMIT = 56 * 1024 * 1024


def _cparams(sem):
    return pltpu.CompilerParams(dimension_semantics=sem, vmem_limit_bytes=VMEM_LIMIT)


def _resident(shape):
    nd = len(shape)
    return pl.BlockSpec(shape, lambda *_: (0,) * nd, pipeline_mode=pl.Buffered(1))


def _rms_rows(x, gain):
    ms = jnp.mean(x * x, axis=-1, keepdims=True)
    return x * lax.rsqrt(ms + EPS) * gain


def _swiglu_residual(x, g_ref, wg_ref, wu_ref, wd_ref, chunk=FFN_CHUNK, side_work=None):
    hn = _rms_rows(x, g_ref[...]).astype(BF16)
    d_ff = wg_ref.shape[1]
    y = None
    for c0 in range(0, d_ff, chunk):
        c1 = min(c0 + chunk, d_ff)
        g = jnp.dot(hn, wg_ref[:, c0:c1], preferred_element_type=F32)
        u = jnp.dot(hn, wu_ref[:, c0:c1], preferred_element_type=F32)
        a = (g / (1.0 + jnp.exp(-g)) * u).astype(BF16)
        yc = jnp.dot(a, wd_ref[c0:c1, :], preferred_element_type=F32)
        y = yc if y is None else y + yc
        if side_work is not None:
            next(side_work, None)
    if side_work is not None:
        for _ in side_work:
            pass
    return x + FFN_HALF * y


def _cast_pieces(src_refs, dst_refs):
    for src, dst in zip(src_refs, dst_refs):
        dst[...] = src[...].astype(BF16)
        yield


def _cast_plan(next_w, n_steps):
    if next_w is None:
        return [], [], [], ()
    stacked, layer = next_w
    in_specs, out_specs, out_shapes = [], [], []
    for w in stacked:
        _, rows, cols = w.shape
        rb = next(r for r in range(BF16_ROWS, rows + 1, BF16_ROWS) if rows % r == 0 and rows // r <= n_steps)
        last = rows // rb - 1
        in_specs.append(pl.BlockSpec((None, rb, cols), lambda s, last=last: (layer, jnp.minimum(s, last), 0)))
        out_specs.append(pl.BlockSpec((rb, cols), lambda s, last=last: (jnp.minimum(s, last), 0)))
        out_shapes.append(jax.ShapeDtypeStruct((rows, cols), BF16))
    return in_specs, out_specs, out_shapes, tuple(stacked)


def _ffn_kernel(*refs, n_cast, with_proj):
    n_in = (7 if with_proj else 5) + n_cast
    ins, o_ref, cast_out = refs[:n_in - n_cast], refs[n_in], refs[n_in + 1:]
    cast_in = refs[n_in - n_cast:n_in]
    if with_proj:
        x_ref, a_ref, wo_ref, g_ref, wg_ref, wu_ref, wd_ref = ins
        x = x_ref[...] + jnp.dot(a_ref[...], wo_ref[...], preferred_element_type=F32)
    else:
        x_ref, g_ref, wg_ref, wu_ref, wd_ref = ins
        x = x_ref[...]
    o_ref[...] = _swiglu_residual(x, g_ref, wg_ref, wu_ref, wd_ref, side_work=_cast_pieces(cast_in, cast_out))


def _ffn(x2, gain, wg, wu, wd, tm, attn_out=None, w_out=None, next_w=None):
    T, D = x2.shape
    F = wg.shape[1]
    n_steps = T // tm
    row_tile = pl.BlockSpec((tm, D), lambda i: (i, 0))
    specs = [row_tile]
    args = (x2,)
    if attn_out is not None:
        specs += [pl.BlockSpec((tm, attn_out.shape[1]), lambda i: (i, 0)), _resident(w_out.shape)]
        args += (attn_out, w_out)
    specs += [_resident((1, D)), _resident((D, F)), _resident((D, F)), _resident((F, D))]
    args += (gain.reshape(1, D), wg, wu, wd)
    cast_in, cast_out, cast_shapes, cast_args = _cast_plan(next_w, n_steps)
    out = pl.pallas_call(
        functools.partial(_ffn_kernel, n_cast=len(cast_args), with_proj=attn_out is not None),
        grid=(n_steps,),
        in_specs=specs + cast_in,
        out_specs=[row_tile] + cast_out,
        out_shape=[jax.ShapeDtypeStruct((T, D), F32)] + cast_shapes,
        compiler_params=_cparams(("arbitrary",)),
        name="ffn" if attn_out is None else "proj_ffn",
    )(*args, *cast_args)
    return out[0], tuple(out[1:])


def _pool_mix(x, halo, first_of_seq, row0, g_ref, w_ref, sc_ref, hbuf, dst_ref):
    tm = x.shape[0]
    gain = g_ref[...]
    hn = _rms_rows(x, gain)
    hh = _rms_rows(halo, gain)
    hbuf[0:POOL_HALO] = jnp.where(first_of_seq, 0.0, hh)
    hbuf[POOL_HALO:POOL_HALO + tm] = hn
    yield
    C = w_ref.shape[1]
    rows = tm // 2
    for g, w_len in enumerate(POOL_WINDOWS):
        cols = slice(g * C, (g + 1) * C)
        for r0 in (0, rows):
            u = hbuf[POOL_HALO + r0:POOL_HALO + r0 + rows, cols]
            wsum = u
            for j in range(1, w_len):
                wsum = wsum + hbuf[POOL_HALO + r0 - j:POOL_HALO + r0 - j + rows, cols]
            t = row0 + r0 + lax.broadcasted_iota(jnp.int32, (rows, 1), 0)
            count = jnp.minimum(t + 1, w_len).astype(F32)
            pooled = (wsum / count - u).astype(BF16)
            y = jnp.dot(pooled, w_ref[g], preferred_element_type=F32)
            dst_ref[r0:r0 + rows, cols] = x[r0:r0 + rows, cols] + y * sc_ref[:, cols]
            yield


def _pool_ffn_kernel(*refs, n_tiles, tiles_per_seq, n_cast):
    x_ref, halo_ref, gmix_ref, pw_ref, psc_ref, g_ref, wg_ref, wu_ref, wd_ref = refs[:9]
    cast_in, o_ref, cast_out = refs[9:9 + n_cast], refs[9 + n_cast], refs[10 + n_cast:10 + 2 * n_cast]
    hbuf, xm0_ref, xm1_ref = refs[10 + 2 * n_cast:]
    s = pl.program_id(0)
    tm = x_ref.shape[0]

    def side(dst_ref=None):
        casts = _cast_pieces(cast_in, cast_out)
        if dst_ref is None:
            return casts
        i = s % tiles_per_seq
        mix = _pool_mix(x_ref[...], halo_ref[...], i == 0, i * tm, gmix_ref, pw_ref, psc_ref, hbuf, dst_ref)
        return itertools.chain(mix, casts)

    def ffn_from(src_ref, side_work):
        o_ref[...] = _swiglu_residual(src_ref[...], g_ref, wg_ref, wu_ref, wd_ref, chunk=POOL_FFN_CHUNK,
                                      side_work=side_work)

    last_ref = xm0_ref if (n_tiles - 1) % 2 == 0 else xm1_ref
    middle = jnp.logical_and(s > 0, s < n_tiles)

    @pl.when(s == 0)
    def _():
        for _ in side(xm0_ref):
            pass

    @pl.when(jnp.logical_and(middle, s % 2 == 1))
    def _():
        ffn_from(xm0_ref, side(xm1_ref))

    @pl.when(jnp.logical_and(middle, s % 2 == 0))
    def _():
        ffn_from(xm1_ref, side(xm0_ref))

    @pl.when(s == n_tiles)
    def _():
        ffn_from(last_ref, side())


def _pool_ffn(x2, seq_len, gmix, pool_w, pool_scale, gain, wg, wu, wd, tm, next_w=None):
    T, D = x2.shape
    F = wg.shape[1]
    G, C, _ = pool_w.shape
    n_tiles = T // tm
    hb = tm // POOL_HALO
    cast_in, cast_out, cast_shapes, cast_args = _cast_plan(next_w, n_tiles + 1)
    kern = functools.partial(_pool_ffn_kernel, n_tiles=n_tiles, tiles_per_seq=seq_len // tm, n_cast=len(cast_args))
    out = pl.pallas_call(
        kern,
        grid=(n_tiles + 1,),
        in_specs=[
            pl.BlockSpec((tm, D), lambda s: (jnp.minimum(s, n_tiles - 1), 0)),
            pl.BlockSpec((POOL_HALO, D), lambda s: (jnp.maximum(jnp.minimum(s, n_tiles - 1) * hb - 1, 0), 0)),
            _resident((1, D)),
            _resident((G, C, C)),
            _resident((1, D)),
            _resident((1, D)),
            _resident((D, F)),
            _resident((D, F)),
            _resident((F, D)),
        ] + cast_in,
        out_specs=[pl.BlockSpec((tm, D), lambda s: (jnp.maximum(s - 1, 0), 0))] + cast_out,
        out_shape=[jax.ShapeDtypeStruct((T, D), F32)] + cast_shapes,
        scratch_shapes=[pltpu.VMEM((POOL_HALO + tm, D), F32), pltpu.VMEM((tm, D), F32), pltpu.VMEM((tm, D), F32)],
        compiler_params=_cparams(("arbitrary",)),
        name="pool_ffn",
    )(x2, x2, gmix.reshape(1, D), pool_w, pool_scale.reshape(1, D), gain.reshape(1, D), wg, wu, wd, *cast_args)
    return out[0], tuple(out[1:])


def _rms_cols(xT, gain_col, n):
    ms = jnp.sum(xT * xT, axis=0, keepdims=True) * (1.0 / n)
    return xT * lax.rsqrt(ms + EPS) * gain_col


def _mla_pre_kernel(bound_ref, x_ref, pos_ref, invf_ref, gmix_ref, winT_ref, gq_ref, wqT_ref, gkv_ref, wkvT_ref,
                    gqh_ref, gkh_ref, qT_ref, k_ref, vT_ref, *, n_heads, q_lora, kv_lora, q_scale):
    tm = x_ref.shape[1]
    half = QK_ROPE // 2
    kv_head = QK_NOPE + V_HEAD

    def sub_tile(t0, tn):
        tok = slice(t0, t0 + tn)
        hn = _rms_rows(x_ref[0, tok, :], gmix_ref[...]).astype(BF16)
        latT = lax.dot_general(winT_ref[...], hn, (((1,), (1,)), ((), ())), preferred_element_type=F32)
        yield
        kpeT = latT[q_lora + kv_lora:]
        cqn = _rms_cols(latT[:q_lora], gq_ref[...], q_lora).astype(BF16)
        ckvn = _rms_cols(latT[q_lora:q_lora + kv_lora], gkv_ref[...], kv_lora).astype(BF16)
        ang = pos_ref[0, :, tok].astype(F32) * invf_ref[...]
        cos, sin = jnp.cos(ang), jnp.sin(ang)
        gqh = gqh_ref[...]
        gkh = gkh_ref[...]
        kpe_ss = jnp.sum(kpeT * kpeT, axis=0, keepdims=True)
        first = lax.broadcasted_iota(jnp.int32, (QK_PAD - QK_HEAD, tn), 0) == 0
        q_pad = jnp.where(first, 1.0, 0.0)
        k_pad = jnp.where(first, -bound_ref[0], 0.0)

        def rope(x1, x2):
            return x1 * cos - x2 * sin, x2 * cos + x1 * sin

        yield
        for h in range(n_heads):
            qh = jnp.dot(wqT_ref[h * QK_HEAD:(h + 1) * QK_HEAD, :], cqn, preferred_element_type=F32)
            kvh = jnp.dot(wkvT_ref[h * kv_head:(h + 1) * kv_head, :], ckvn, preferred_element_type=F32)
            ms = jnp.sum(qh * qh, axis=0, keepdims=True) * (1.0 / QK_HEAD)
            qn = qh * (lax.rsqrt(ms + EPS) * q_scale) * gqh
            r1, r2 = rope(qn[QK_NOPE:QK_NOPE + half], qn[QK_NOPE + half:])
            qT_ref[0, h, :, tok] = jnp.concatenate([qn[:QK_NOPE], r1, r2, q_pad], axis=0).astype(BF16)

            kn = kvh[:QK_NOPE]
            ms = (jnp.sum(kn * kn, axis=0, keepdims=True) + kpe_ss) * (1.0 / QK_HEAD)
            rs = lax.rsqrt(ms + EPS)
            r1, r2 = rope(kpeT[:half] * rs * gkh[QK_NOPE:QK_NOPE + half],
                          kpeT[half:] * rs * gkh[QK_NOPE + half:])
            k_ref[0, h, 0, tok, :QK_NOPE] = (kn * rs * gkh[:QK_NOPE]).T.astype(BF16)
            k_ref[0, h, 0, tok, QK_NOPE:] = jnp.concatenate([r1, r2, k_pad], axis=0).T.astype(BF16)
            vT_ref[0, h, 0, :, tok] = kvh[QK_NOPE:].astype(BF16)
            yield

    n_sub = MLA_SUB_TILES if tm % (MLA_SUB_TILES * 128) == 0 else 1
    stages = [sub_tile(i * (tm // n_sub), tm // n_sub) for i in range(n_sub)]
    live = list(stages)
    lead = 0
    while live:
        lead += 1
        for g in list(live[:lead]):
            if next(g, StopIteration) is StopIteration:
                live.remove(g)
                lead -= 1


def _mla_pre(bound, x3, pos3, invf, gmix, winT, gq, wqT, gkv, wkvT, gqh, gkh, n_heads, tm):
    B, S, D = x3.shape
    q_lora, kv_lora = wqT.shape[1], wkvT.shape[1]
    n_t = S // tm
    kern = functools.partial(_mla_pre_kernel, n_heads=n_heads, q_lora=q_lora, kv_lora=kv_lora,
                             q_scale=Q_SCALE)
    return pl.pallas_call(
        kern,
        grid=(B, n_t),
        in_specs=[
            pl.BlockSpec(memory_space=pltpu.SMEM),
            pl.BlockSpec((1, tm, D), lambda b, i: (b, i, 0)),
            pl.BlockSpec((1, 1, tm), lambda b, i: (b, 0, i)),
            _resident(invf.shape),
            _resident((1, D)),
            _resident(winT.shape),
            _resident((q_lora, 1)),
            _resident(wqT.shape),
            _resident((kv_lora, 1)),
            _resident(wkvT.shape),
            _resident((QK_HEAD, 1)),
            _resident((QK_HEAD, 1)),
        ],
        out_specs=[
            pl.BlockSpec((1, n_heads, QK_PAD, tm), lambda b, i: (b, 0, 0, i)),
            pl.BlockSpec((1, n_heads, 1, tm, QK_PAD), lambda b, i: (b, 0, i, 0, 0)),
            pl.BlockSpec((1, n_heads, 1, V_HEAD, tm), lambda b, i: (b, 0, i, 0, 0)),
        ],
        out_shape=[
            jax.ShapeDtypeStruct((B, n_heads, QK_PAD, S), BF16),
            jax.ShapeDtypeStruct((B, n_heads, n_t, tm, QK_PAD), BF16),
            jax.ShapeDtypeStruct((B, n_heads, n_t, V_HEAD, tm), BF16),
        ],
        compiler_params=_cparams(("parallel", "parallel")),
        name="mla_pre",
    )(bound, x3, pos3, invf, gmix.reshape(1, D), winT, gq.reshape(-1, 1), wqT, gkv.reshape(-1, 1), wkvT,
      gqh.reshape(-1, 1), gkh.reshape(-1, 1))


def _attn_kernel(safe_ref, qT_ref, k_ref, vT_ref, o_ref, m_ref, l_ref, acc_ref, p0_ref, p1_ref):
    i = pl.program_id(2)
    sub = l_ref.shape[0]
    tk, tq = p0_ref.shape
    v_w = vT_ref.shape[-1]
    l_ref[...] = jnp.zeros(l_ref.shape, F32)
    acc_ref[...] = jnp.zeros(acc_ref.shape, F32)

    def scores(j, diagonal):
        sT = jnp.dot(k_ref[0, 0, j], qT_ref[0, 0], preferred_element_type=F32)
        if not diagonal:
            return sT
        key = lax.broadcasted_iota(jnp.int32, sT.shape, 0)
        qry = lax.broadcasted_iota(jnp.int32, sT.shape, 1)
        return jnp.where(key <= qry, sT, -jnp.inf)

    def partial_sum(p):
        return jnp.sum(p.reshape(p.shape[0] // sub, sub, p.shape[1]), axis=0)

    def probs(j, p_ref, diagonal=False):
        p = jnp.exp2(scores(j, diagonal))
        l_ref[...] += partial_sum(p)
        p_ref[...] = p.astype(BF16)

    def pv_value(j, p):
        out = None
        for c in range(tk // v_w):
            part = jnp.dot(vT_ref[0, 0, j * (tk // v_w) + c], p[c * v_w:(c + 1) * v_w, :],
                           preferred_element_type=F32)
            out = part if out is None else out + part
        return out

    def pv(j, p_ref):
        acc_ref[...] += pv_value(j, p_ref)

    def online(j, diagonal=False):
        sT = scores(j, diagonal)
        m_old = m_ref[...]
        m_new = jnp.maximum(m_old, jnp.max(sT, axis=0, keepdims=True))
        alpha = jnp.exp2(m_old - m_new)
        p = jnp.exp2(sT - m_new)
        l_ref[...] = alpha * l_ref[...] + partial_sum(p)
        acc_ref[...] = alpha * acc_ref[...] + pv_value(j, p.astype(BF16))
        m_ref[...] = m_new

    @pl.when(safe_ref[0] != 0)
    def _():
        probs(i, p0_ref, diagonal=True)

        def pair(t):
            probs(2 * t, p1_ref)
            pv(jnp.where(t == 0, i, 2 * t - 1), p0_ref)
            probs(2 * t + 1, p0_ref)
            pv(2 * t, p1_ref)

        def two_pairs(u, carry):
            pair(2 * u)
            pair(2 * u + 1)
            return carry

        n_pairs = i // 2
        lax.fori_loop(0, n_pairs // 2, two_pairs, 0)

        @pl.when(n_pairs % 2 == 1)
        def _():
            pair(n_pairs - 1)

        @pl.when(i % 2 == 0)
        def _():
            pv(jnp.maximum(i - 1, 0), p0_ref)

        @pl.when(i % 2 == 1)
        def _():
            probs(i - 1, p1_ref)
            pv(jnp.where(i == 1, i, i - 2), p0_ref)
            pv(i - 1, p1_ref)

    @pl.when(safe_ref[0] == 0)
    def _():
        m_ref[...] = jnp.full(m_ref.shape, -jnp.inf, F32)

        def body(j, carry):
            online(j)
            return carry

        lax.fori_loop(0, i, body, 0)
        online(i, diagonal=True)

    l = jnp.sum(l_ref[...], axis=0, keepdims=True)
    o_ref[0] = (acc_ref[...] / l).T.astype(o_ref.dtype)


def _attn(safe, qT, k, vT):
    B, H, _, S = qT.shape
    n_v, v_w = vT.shape[2], vT.shape[4]
    tq = ATTN_TILE if S % ATTN_TILE == 0 else v_w
    assert S % tq == 0 and tq % v_w == 0, (S, tq, v_w)
    k = k.reshape(B, H, S // tq, tq, QK_PAD)
    return pl.pallas_call(
        _attn_kernel,
        grid=(B, H, S // tq),
        in_specs=[
            pl.BlockSpec(memory_space=pltpu.SMEM),
            pl.BlockSpec((1, 1, QK_PAD, tq), lambda b, h, i: (b, h, 0, i)),
            pl.BlockSpec((1, 1, S // tq, tq, QK_PAD), lambda b, h, i: (b, h, 0, 0, 0)),
            pl.BlockSpec((1, 1, n_v, V_HEAD, v_w), lambda b, h, i: (b, h, 0, 0, 0)),
        ],
        out_specs=pl.BlockSpec((1, tq, V_HEAD), lambda b, h, i: (b, i, h)),
        out_shape=jax.ShapeDtypeStruct((B, S, H * V_HEAD), BF16),
        scratch_shapes=[pltpu.VMEM((1, tq), F32), pltpu.VMEM((SUBLANES, tq), F32),
                        pltpu.VMEM((V_HEAD, tq), F32), pltpu.VMEM((tq, tq), BF16), pltpu.VMEM((tq, tq), BF16)],
        compiler_params=_cparams(("parallel", "parallel", "arbitrary")),
        name="attn",
    )(safe, qT, k, vT)


def _tile(n, want):
    t = min(n, want)
    assert n % t == 0, (n, t)
    return t


def kernel(x, positions, ffn1_norm, ffn1_w_gate, ffn1_w_up, ffn1_w_down, mix_norm, pool_w, pool_scale,
           mla_w_in, mla_q_norm, mla_w_q_up, mla_kv_norm, mla_w_kv_up, mla_q_head_norm, mla_k_head_norm,
           mla_w_out, ffn2_norm, ffn2_w_gate, ffn2_w_up, ffn2_w_down):
    B, S, D = x.shape
    depth = ffn1_norm.shape[0]
    n_heads = mla_w_out.shape[1] // V_HEAD
    T = B * S
    tm_ffn = _tile(T, 1024)
    tm_pool = _tile(S, 512)
    tk = _tile(S, 1024)

    invf = (1.0 / (ROPE_THETA ** (jnp.arange(0, QK_ROPE, 2, dtype=F32) / QK_ROPE))).reshape(-1, 1)
    pos3 = positions.reshape(B, 1, S)

    ffn1_w = (ffn1_w_gate, ffn1_w_up, ffn1_w_down)
    ffn2_w = (ffn2_w_gate, ffn2_w_up, ffn2_w_down)
    w1 = tuple(w[0].astype(BF16) for w in ffn1_w)
    x = x.reshape(T, D)
    for i in range(depth):
        x, w2 = _ffn(x, ffn1_norm[i], *w1, tm_ffn, next_w=(ffn2_w, i))
        j = i // 2
        after = (ffn1_w, i + 1) if i + 1 < depth else None
        if i % 2 == 0:
            x, w1 = _pool_ffn(x, S, mix_norm[i], pool_w[j].astype(BF16), pool_scale[j], ffn2_norm[i], *w2,
                              tm_pool, next_w=after)
        else:
            bound = (Q_SCALE * QK_HEAD * jnp.max(jnp.abs(mla_q_head_norm[j]))
                     * jnp.max(jnp.abs(mla_k_head_norm[j]))).reshape(1)
            safe = (2.0 * bound <= F32_SAFE_EXP2).astype(jnp.int32)
            qT, k, vT = _mla_pre(bound, x.reshape(B, S, D), pos3, invf, mix_norm[i], mla_w_in[j].T.astype(BF16),
                                 mla_q_norm[j], mla_w_q_up[j].T.astype(BF16), mla_kv_norm[j],
                                 mla_w_kv_up[j].T.astype(BF16), mla_q_head_norm[j], mla_k_head_norm[j], n_heads, tk)
            o = _attn(safe, qT, k, vT)
            x, w1 = _ffn(x, ffn2_norm[i], *w2, tm_ffn, attn_out=o.reshape(T, n_heads * V_HEAD),
                         w_out=mla_w_out[j].astype(BF16), next_w=after)
    return x.reshape(B, S, D)
```

```python
import functools
import itertools
import math

import jax
import jax.numpy as jnp
from jax import lax
from jax.experimental import pallas as pl
from jax.experimental.pallas import tpu as pltpu

F32 = jnp.float32
BF16 = jnp.bfloat16

EPS = 1e-6
FFN_HALF = 0.5
FFN_CHUNK = 256
ATTN_TILE = 1024
MLA_HEADS_PER_DOT = 2
MLA_SUB_TILES = 4
POOL_FFN_CHUNK = 256
POOL_WINDOWS = (2, 4, 8, 16)
POOL_HALO = 16
QK_NOPE = 128
QK_ROPE = 64
QK_HEAD = QK_NOPE + QK_ROPE
QK_PAD = 256
V_HEAD = 128
ROPE_THETA = 10000.0
LOG2E = math.log2(math.e)
Q_SCALE = QK_HEAD ** -0.5 * LOG2E
SUBLANES = 8
BF16_ROWS = 16
F32_SAFE_EXP2 = 120.0

VMEM_LIMIT = 56 * 1024 * 1024


def _cparams(sem):
    return pltpu.CompilerParams(dimension_semantics=sem, vmem_limit_bytes=VMEM_LIMIT)


def _resident(shape):
    nd = len(shape)
    return pl.BlockSpec(shape, lambda *_: (0,) * nd, pipeline_mode=pl.Buffered(1))


def _rms_rows(x, gain):
    ms = jnp.mean(x * x, axis=-1, keepdims=True)
    return x * lax.rsqrt(ms + EPS) * gain


def _swiglu_residual(x, g_ref, wg_ref, wu_ref, wd_ref, chunk=FFN_CHUNK, side_work=None):
    hn = _rms_rows(x, g_ref[...]).astype(BF16)
    d_ff = wg_ref.shape[1]
    y = None
    for c0 in range(0, d_ff, chunk):
        c1 = min(c0 + chunk, d_ff)
        g = jnp.dot(hn, wg_ref[:, c0:c1], preferred_element_type=F32)
        u = jnp.dot(hn, wu_ref[:, c0:c1], preferred_element_type=F32)
        a = (g / (1.0 + jnp.exp(-g)) * u).astype(BF16)
        yc = jnp.dot(a, wd_ref[c0:c1, :], preferred_element_type=F32)
        y = yc if y is None else y + yc
        if side_work is not None:
            next(side_work, None)
    if side_work is not None:
        for _ in side_work:
            pass
    return x + FFN_HALF * y


def _cast_pieces(src_refs, dst_refs):
    for src, dst in zip(src_refs, dst_refs):
        dst[...] = src[...].astype(BF16)
        yield


def _cast_plan(next_w, n_steps):
    if next_w is None:
        return [], [], [], ()
    stacked, layer = next_w
    in_specs, out_specs, out_shapes = [], [], []
    for w in stacked:
        _, rows, cols = w.shape
        rb = next(r for r in range(BF16_ROWS, rows + 1, BF16_ROWS) if rows % r == 0 and rows // r <= n_steps)
        last = rows // rb - 1
        in_specs.append(pl.BlockSpec((None, rb, cols), lambda s, last=last: (layer, jnp.minimum(s, last), 0)))
        out_specs.append(pl.BlockSpec((rb, cols), lambda s, last=last: (jnp.minimum(s, last), 0)))
        out_shapes.append(jax.ShapeDtypeStruct((rows, cols), BF16))
    return in_specs, out_specs, out_shapes, tuple(stacked)


def _ffn_kernel(*refs, n_cast, with_proj):
    n_in = (7 if with_proj else 5) + n_cast
    ins, o_ref, cast_out = refs[:n_in - n_cast], refs[n_in], refs[n_in + 1:]
    cast_in = refs[n_in - n_cast:n_in]
    if with_proj:
        x_ref, a_ref, wo_ref, g_ref, wg_ref, wu_ref, wd_ref = ins
        x = x_ref[...] + jnp.dot(a_ref[...], wo_ref[...], preferred_element_type=F32)
    else:
        x_ref, g_ref, wg_ref, wu_ref, wd_ref = ins
        x = x_ref[...]
    o_ref[...] = _swiglu_residual(x, g_ref, wg_ref, wu_ref, wd_ref, side_work=_cast_pieces(cast_in, cast_out))


def _ffn(x2, gain, wg, wu, wd, tm, attn_out=None, w_out=None, next_w=None):
    T, D = x2.shape
    F = wg.shape[1]
    n_steps = T // tm
    row_tile = pl.BlockSpec((tm, D), lambda i: (i, 0))
    specs = [row_tile]
    args = (x2,)
    if attn_out is not None:
        specs += [pl.BlockSpec((tm, attn_out.shape[1]), lambda i: (i, 0)), _resident(w_out.shape)]
        args += (attn_out, w_out)
    specs += [_resident((1, D)), _resident((D, F)), _resident((D, F)), _resident((F, D))]
    args += (gain.reshape(1, D), wg, wu, wd)
    cast_in, cast_out, cast_shapes, cast_args = _cast_plan(next_w, n_steps)
    out = pl.pallas_call(
        functools.partial(_ffn_kernel, n_cast=len(cast_args), with_proj=attn_out is not None),
        grid=(n_steps,),
        in_specs=specs + cast_in,
        out_specs=[row_tile] + cast_out,
        out_shape=[jax.ShapeDtypeStruct((T, D), F32)] + cast_shapes,
        compiler_params=_cparams(("arbitrary",)),
        name="ffn" if attn_out is None else "proj_ffn",
    )(*args, *cast_args)
    return out[0], tuple(out[1:])


def _pool_mix(x, halo, first_of_seq, row0, g_ref, w_ref, sc_ref, hbuf, dst_ref):
    tm = x.shape[0]
    gain = g_ref[...]
    hn = _rms_rows(x, gain)
    hh = _rms_rows(halo, gain)
    hbuf[0:POOL_HALO] = jnp.where(first_of_seq, 0.0, hh)
    hbuf[POOL_HALO:POOL_HALO + tm] = hn
    yield
    C = w_ref.shape[1]
    rows = tm // 2
    for g, w_len in enumerate(POOL_WINDOWS):
        cols = slice(g * C, (g + 1) * C)
        for r0 in (0, rows):
            u = hbuf[POOL_HALO + r0:POOL_HALO + r0 + rows, cols]
            wsum = u
            for j in range(1, w_len):
                wsum = wsum + hbuf[POOL_HALO + r0 - j:POOL_HALO + r0 - j + rows, cols]
            t = row0 + r0 + lax.broadcasted_iota(jnp.int32, (rows, 1), 0)
            count = jnp.minimum(t + 1, w_len).astype(F32)
            pooled = (wsum / count - u).astype(BF16)
            y = jnp.dot(pooled, w_ref[g], preferred_element_type=F32)
            dst_ref[r0:r0 + rows, cols] = x[r0:r0 + rows, cols] + y * sc_ref[:, cols]
            yield


def _pool_ffn_kernel(*refs, n_tiles, tiles_per_seq, n_cast):
    x_ref, halo_ref, gmix_ref, pw_ref, psc_ref, g_ref, wg_ref, wu_ref, wd_ref = refs[:9]
    cast_in, o_ref, cast_out = refs[9:9 + n_cast], refs[9 + n_cast], refs[10 + n_cast:10 + 2 * n_cast]
    hbuf, xm0_ref, xm1_ref = refs[10 + 2 * n_cast:]
    s = pl.program_id(0)
    tm = x_ref.shape[0]

    def side(dst_ref=None):
        casts = _cast_pieces(cast_in, cast_out)
        if dst_ref is None:
            return casts
        i = s % tiles_per_seq
        mix = _pool_mix(x_ref[...], halo_ref[...], i == 0, i * tm, gmix_ref, pw_ref, psc_ref, hbuf, dst_ref)
        return itertools.chain(mix, casts)

    def ffn_from(src_ref, side_work):
        o_ref[...] = _swiglu_residual(src_ref[...], g_ref, wg_ref, wu_ref, wd_ref, chunk=POOL_FFN_CHUNK,
                                      side_work=side_work)

    last_ref = xm0_ref if (n_tiles - 1) % 2 == 0 else xm1_ref
    middle = jnp.logical_and(s > 0, s < n_tiles)

    @pl.when(s == 0)
    def _():
        for _ in side(xm0_ref):
            pass

    @pl.when(jnp.logical_and(middle, s % 2 == 1))
    def _():
        ffn_from(xm0_ref, side(xm1_ref))

    @pl.when(jnp.logical_and(middle, s % 2 == 0))
    def _():
        ffn_from(xm1_ref, side(xm0_ref))

    @pl.when(s == n_tiles)
    def _():
        ffn_from(last_ref, side())


def _pool_ffn(x2, seq_len, gmix, pool_w, pool_scale, gain, wg, wu, wd, tm, next_w=None):
    T, D = x2.shape
    F = wg.shape[1]
    G, C, _ = pool_w.shape
    n_tiles = T // tm
    hb = tm // POOL_HALO
    cast_in, cast_out, cast_shapes, cast_args = _cast_plan(next_w, n_tiles + 1)
    kern = functools.partial(_pool_ffn_kernel, n_tiles=n_tiles, tiles_per_seq=seq_len // tm, n_cast=len(cast_args))
    out = pl.pallas_call(
        kern,
        grid=(n_tiles + 1,),
        in_specs=[
            pl.BlockSpec((tm, D), lambda s: (jnp.minimum(s, n_tiles - 1), 0)),
            pl.BlockSpec((POOL_HALO, D), lambda s: (jnp.maximum(jnp.minimum(s, n_tiles - 1) * hb - 1, 0), 0)),
            _resident((1, D)),
            _resident((G, C, C)),
            _resident((1, D)),
            _resident((1, D)),
            _resident((D, F)),
            _resident((D, F)),
            _resident((F, D)),
        ] + cast_in,
        out_specs=[pl.BlockSpec((tm, D), lambda s: (jnp.maximum(s - 1, 0), 0))] + cast_out,
        out_shape=[jax.ShapeDtypeStruct((T, D), F32)] + cast_shapes,
        scratch_shapes=[pltpu.VMEM((POOL_HALO + tm, D), F32), pltpu.VMEM((tm, D), F32), pltpu.VMEM((tm, D), F32)],
        compiler_params=_cparams(("arbitrary",)),
        name="pool_ffn",
    )(x2, x2, gmix.reshape(1, D), pool_w, pool_scale.reshape(1, D), gain.reshape(1, D), wg, wu, wd, *cast_args)
    return out[0], tuple(out[1:])


def _rms_cols(xT, gain_col, n):
    ms = jnp.sum(xT * xT, axis=0, keepdims=True) * (1.0 / n)
    return xT * lax.rsqrt(ms + EPS) * gain_col


def _mla_pre_kernel(bound_ref, x_ref, pos_ref, invf_ref, gmix_ref, winT_ref, gq_ref, wqT_ref, gkv_ref, wkvT_ref,
                    gqh_ref, gkh_ref, qT_ref, k_ref, vT_ref, *, n_heads, q_lora, kv_lora, q_scale):
    tm = x_ref.shape[1]
    half = QK_ROPE // 2
    kv_head = QK_NOPE + V_HEAD

    def sub_tile(t0, tn):
        tok = slice(t0, t0 + tn)
        hn = _rms_rows(x_ref[0, tok, :], gmix_ref[...]).astype(BF16)
        latT = lax.dot_general(winT_ref[...], hn, (((1,), (1,)), ((), ())), preferred_element_type=F32)
        yield
        kpeT = latT[q_lora + kv_lora:]
        cqn = _rms_cols(latT[:q_lora], gq_ref[...], q_lora).astype(BF16)
        ckvn = _rms_cols(latT[q_lora:q_lora + kv_lora], gkv_ref[...], kv_lora).astype(BF16)
        ang = pos_ref[0, :, tok].astype(F32) * invf_ref[...]
        cos, sin = jnp.cos(ang), jnp.sin(ang)
        gqh = gqh_ref[...]
        gkh = gkh_ref[...]
        kpe_ss = jnp.sum(kpeT * kpeT, axis=0, keepdims=True)
        first = lax.broadcasted_iota(jnp.int32, (QK_PAD - QK_HEAD, tn), 0) == 0
        q_pad = jnp.where(first, 1.0, 0.0)
        k_pad = jnp.where(first, -bound_ref[0], 0.0)

        def rope(x1, x2):
            return x1 * cos - x2 * sin, x2 * cos + x1 * sin

        yield
        for h0 in range(0, n_heads, MLA_HEADS_PER_DOT):
            hs = MLA_HEADS_PER_DOT
            qg = jnp.dot(wqT_ref[h0 * QK_HEAD:(h0 + hs) * QK_HEAD, :], cqn, preferred_element_type=F32)
            kvg = jnp.dot(wkvT_ref[h0 * kv_head:(h0 + hs) * kv_head, :], ckvn, preferred_element_type=F32)
            for h in range(h0, h0 + hs):
                qh = qg[(h - h0) * QK_HEAD:(h - h0 + 1) * QK_HEAD]
                kvh = kvg[(h - h0) * kv_head:(h - h0 + 1) * kv_head]
                ms = jnp.sum(qh * qh, axis=0, keepdims=True) * (1.0 / QK_HEAD)
                qn = qh * (lax.rsqrt(ms + EPS) * q_scale) * gqh
                r1, r2 = rope(qn[QK_NOPE:QK_NOPE + half], qn[QK_NOPE + half:])
                qT_ref[0, h, :, tok] = jnp.concatenate([qn[:QK_NOPE], r1, r2, q_pad], axis=0).astype(BF16)

                kn = kvh[:QK_NOPE]
                ms = (jnp.sum(kn * kn, axis=0, keepdims=True) + kpe_ss) * (1.0 / QK_HEAD)
                rs = lax.rsqrt(ms + EPS)
                r1, r2 = rope(kpeT[:half] * rs * gkh[QK_NOPE:QK_NOPE + half],
                              kpeT[half:] * rs * gkh[QK_NOPE + half:])
                k_ref[0, h, 0, tok, :QK_NOPE] = (kn * rs * gkh[:QK_NOPE]).T.astype(BF16)
                k_ref[0, h, 0, tok, QK_NOPE:] = jnp.concatenate([r1, r2, k_pad], axis=0).T.astype(BF16)
                vT_ref[0, h, 0, :, tok] = kvh[QK_NOPE:].astype(BF16)
                yield

    n_sub = MLA_SUB_TILES if tm % (MLA_SUB_TILES * 128) == 0 else 1
    stages = [sub_tile(i * (tm // n_sub), tm // n_sub) for i in range(n_sub)]
    live = list(stages)
    lead = 0
    while live:
        lead += 1
        for g in list(live[:lead]):
            if next(g, StopIteration) is StopIteration:
                live.remove(g)
                lead -= 1


def _mla_pre(bound, x3, pos3, invf, gmix, winT, gq, wqT, gkv, wkvT, gqh, gkh, n_heads, tm):
    B, S, D = x3.shape
    q_lora, kv_lora = wqT.shape[1], wkvT.shape[1]
    n_t = S // tm
    kern = functools.partial(_mla_pre_kernel, n_heads=n_heads, q_lora=q_lora, kv_lora=kv_lora,
                             q_scale=Q_SCALE)
    return pl.pallas_call(
        kern,
        grid=(B, n_t),
        in_specs=[
            pl.BlockSpec(memory_space=pltpu.SMEM),
            pl.BlockSpec((1, tm, D), lambda b, i: (b, i, 0)),
            pl.BlockSpec((1, 1, tm), lambda b, i: (b, 0, i)),
            _resident(invf.shape),
            _resident((1, D)),
            _resident(winT.shape),
            _resident((q_lora, 1)),
            _resident(wqT.shape),
            _resident((kv_lora, 1)),
            _resident(wkvT.shape),
            _resident((QK_HEAD, 1)),
            _resident((QK_HEAD, 1)),
        ],
        out_specs=[
            pl.BlockSpec((1, n_heads, QK_PAD, tm), lambda b, i: (b, 0, 0, i)),
            pl.BlockSpec((1, n_heads, 1, tm, QK_PAD), lambda b, i: (b, 0, i, 0, 0)),
            pl.BlockSpec((1, n_heads, 1, V_HEAD, tm), lambda b, i: (b, 0, i, 0, 0)),
        ],
        out_shape=[
            jax.ShapeDtypeStruct((B, n_heads, QK_PAD, S), BF16),
            jax.ShapeDtypeStruct((B, n_heads, n_t, tm, QK_PAD), BF16),
            jax.ShapeDtypeStruct((B, n_heads, n_t, V_HEAD, tm), BF16),
        ],
        compiler_params=_cparams(("parallel", "parallel")),
        name="mla_pre",
    )(bound, x3, pos3, invf, gmix.reshape(1, D), winT, gq.reshape(-1, 1), wqT, gkv.reshape(-1, 1), wkvT,
      gqh.reshape(-1, 1), gkh.reshape(-1, 1))


def _attn_kernel(safe_ref, qT_ref, k_ref, vT_ref, o_ref, m_ref, l_ref, acc_ref, p0_ref, p1_ref):
    i = pl.program_id(2)
    sub = l_ref.shape[0]
    tk, tq = p0_ref.shape
    v_w = vT_ref.shape[-1]
    l_ref[...] = jnp.zeros(l_ref.shape, F32)
    acc_ref[...] = jnp.zeros(acc_ref.shape, F32)

    def scores(j, diagonal):
        sT = jnp.dot(k_ref[0, 0, j], qT_ref[0, 0], preferred_element_type=F32)
        if not diagonal:
            return sT
        key = lax.broadcasted_iota(jnp.int32, sT.shape, 0)
        qry = lax.broadcasted_iota(jnp.int32, sT.shape, 1)
        return jnp.where(key <= qry, sT, -jnp.inf)

    def partial_sum(p):
        return jnp.sum(p.reshape(p.shape[0] // sub, sub, p.shape[1]), axis=0)

    def probs(j, p_ref):
        p = jnp.exp2(scores(j, False))
        l_ref[...] += partial_sum(p)
        p_ref[...] = p.astype(BF16)

    def probs_diagonal(j, p_ref):
        h = tk // 2

        def causal(sT):
            key = lax.broadcasted_iota(jnp.int32, sT.shape, 0)
            qry = lax.broadcasted_iota(jnp.int32, sT.shape, 1)
            return jnp.where(key <= qry, sT, -jnp.inf)

        kt = k_ref[0, 0, j]
        top = jnp.dot(kt[:h], qT_ref[0, 0], preferred_element_type=F32)
        p_top = jnp.exp2(causal(top))
        bot = jnp.dot(kt[h:], qT_ref[0, 0, :, h:], preferred_element_type=F32)
        p_bot = jnp.exp2(causal(bot))
        l_ref[...] += partial_sum(p_top)
        l_ref[:, h:] += partial_sum(p_bot)
        p_ref[:h, :] = p_top.astype(BF16)
        p_ref[h:, :h] = jnp.zeros((tk - h, h), BF16)
        p_ref[h:, h:] = p_bot.astype(BF16)

    def pv_value(j, p):
        out = None
        for c in range(tk // v_w):
            part = jnp.dot(vT_ref[0, 0, j * (tk // v_w) + c], p[c * v_w:(c + 1) * v_w, :],
                           preferred_element_type=F32)
            out = part if out is None else out + part
        return out

    def pv(j, p_ref):
        acc_ref[...] += pv_value(j, p_ref)

    def online(j, diagonal=False):
        sT = scores(j, diagonal)
        m_old = m_ref[...]
        m_new = jnp.maximum(m_old, jnp.max(sT, axis=0, keepdims=True))
        alpha = jnp.exp2(m_old - m_new)
        p = jnp.exp2(sT - m_new)
        l_ref[...] = alpha * l_ref[...] + partial_sum(p)
        acc_ref[...] = alpha * acc_ref[...] + pv_value(j, p.astype(BF16))
        m_ref[...] = m_new

    @pl.when(safe_ref[0] != 0)
    def _():
        probs_diagonal(i, p0_ref)

        def pair(t):
            probs(2 * t, p1_ref)
            pv(jnp.where(t == 0, i, 2 * t - 1), p0_ref)
            probs(2 * t + 1, p0_ref)
            pv(2 * t, p1_ref)

        def two_pairs(u, carry):
            pair(2 * u)
            pair(2 * u + 1)
            return carry

        n_pairs = i // 2
        lax.fori_loop(0, n_pairs // 2, two_pairs, 0)

        @pl.when(n_pairs % 2 == 1)
        def _():
            pair(n_pairs - 1)

        @pl.when(i % 2 == 0)
        def _():
            pv(jnp.maximum(i - 1, 0), p0_ref)

        @pl.when(i % 2 == 1)
        def _():
            probs(i - 1, p1_ref)
            pv(jnp.where(i == 1, i, i - 2), p0_ref)
            pv(i - 1, p1_ref)

    @pl.when(safe_ref[0] == 0)
    def _():
        m_ref[...] = jnp.full(m_ref.shape, -jnp.inf, F32)

        def body(j, carry):
            online(j)
            return carry

        lax.fori_loop(0, i, body, 0)
        online(i, diagonal=True)

    l = jnp.sum(l_ref[...], axis=0, keepdims=True)
    o_ref[0] = (acc_ref[...] / l).T.astype(o_ref.dtype)


def _attn(safe, qT, k, vT):
    B, H, _, S = qT.shape
    n_v, v_w = vT.shape[2], vT.shape[4]
    tq = ATTN_TILE if S % ATTN_TILE == 0 else v_w
    assert S % tq == 0 and tq % v_w == 0, (S, tq, v_w)
    k = k.reshape(B, H, S // tq, tq, QK_PAD)
    return pl.pallas_call(
        _attn_kernel,
        grid=(B, H, S // tq),
        in_specs=[
            pl.BlockSpec(memory_space=pltpu.SMEM),
            pl.BlockSpec((1, 1, QK_PAD, tq), lambda b, h, i: (b, h, 0, i)),
            pl.BlockSpec((1, 1, S // tq, tq, QK_PAD), lambda b, h, i: (b, h, 0, 0, 0)),
            pl.BlockSpec((1, 1, n_v, V_HEAD, v_w), lambda b, h, i: (b, h, 0, 0, 0)),
        ],
        out_specs=pl.BlockSpec((1, tq, V_HEAD), lambda b, h, i: (b, i, h)),
        out_shape=jax.ShapeDtypeStruct((B, S, H * V_HEAD), BF16),
        scratch_shapes=[pltpu.VMEM((1, tq), F32), pltpu.VMEM((SUBLANES, tq), F32),
                        pltpu.VMEM((V_HEAD, tq), F32), pltpu.VMEM((tq, tq), BF16), pltpu.VMEM((tq, tq), BF16)],
        compiler_params=_cparams(("parallel", "parallel", "arbitrary")),
        name="attn",
    )(safe, qT, k, vT)


def _tile(n, want):
    t = min(n, want)
    assert n % t == 0, (n, t)
    return t


def kernel(x, positions, ffn1_norm, ffn1_w_gate, ffn1_w_up, ffn1_w_down, mix_norm, pool_w, pool_scale,
           mla_w_in, mla_q_norm, mla_w_q_up, mla_kv_norm, mla_w_kv_up, mla_q_head_norm, mla_k_head_norm,
           mla_w_out, ffn2_norm, ffn2_w_gate, ffn2_w_up, ffn2_w_down):
    B, S, D = x.shape
    depth = ffn1_norm.shape[0]
    n_heads = mla_w_out.shape[1] // V_HEAD
    T = B * S
    tm_ffn = _tile(T, 1024)
    tm_pool = _tile(S, 512)
    tk = _tile(S, 1024)

    invf = (1.0 / (ROPE_THETA ** (jnp.arange(0, QK_ROPE, 2, dtype=F32) / QK_ROPE))).reshape(-1, 1)
    pos3 = positions.reshape(B, 1, S)

    ffn1_w = (ffn1_w_gate, ffn1_w_up, ffn1_w_down)
    ffn2_w = (ffn2_w_gate, ffn2_w_up, ffn2_w_down)
    w1 = tuple(w[0].astype(BF16) for w in ffn1_w)
    x = x.reshape(T, D)
    for i in range(depth):
        x, w2 = _ffn(x, ffn1_norm[i], *w1, tm_ffn, next_w=(ffn2_w, i))
        j = i // 2
        after = (ffn1_w, i + 1) if i + 1 < depth else None
        if i % 2 == 0:
            x, w1 = _pool_ffn(x, S, mix_norm[i], pool_w[j].astype(BF16), pool_scale[j], ffn2_norm[i], *w2,
                              tm_pool, next_w=after)
        else:
            bound = (Q_SCALE * QK_HEAD * jnp.max(jnp.abs(mla_q_head_norm[j]))
                     * jnp.max(jnp.abs(mla_k_head_norm[j]))).reshape(1)
            safe = (2.0 * bound <= F32_SAFE_EXP2).astype(jnp.int32)
            qT, k, vT = _mla_pre(bound, x.reshape(B, S, D), pos3, invf, mix_norm[i], mla_w_in[j].T.astype(BF16),
                                 mla_q_norm[j], mla_w_q_up[j].T.astype(BF16), mla_kv_norm[j],
                                 mla_w_kv_up[j].T.astype(BF16), mla_q_head_norm[j], mla_k_head_norm[j], n_heads, tk)
            o = _attn(safe, qT, k, vT)
            x, w1 = _ffn(x, ffn2_norm[i], *w2, tm_ffn, attn_out=o.reshape(T, n_heads * V_HEAD),
                         w_out=mla_w_out[j].astype(BF16), next_w=after)
    return x.reshape(B, S, D)
```

```python
import functools
import itertools
import math

import jax
import jax.numpy as jnp
from jax import lax
from jax.experimental import pallas as pl
from jax.experimental.pallas import tpu as pltpu

F32 = jnp.float32
BF16 = jnp.bfloat16

EPS = 1e-6
FFN_HALF = 0.5
FFN_CHUNK = 256
ATTN_TILE = 1024
MLA_HEADS_PER_DOT = 2
MLA_SUB_TILES = 4
POOL_FFN_CHUNK = 256
POOL_WINDOWS = (2, 4, 8, 16)
POOL_HALO = 16
QK_NOPE = 128
QK_ROPE = 64
QK_HEAD = QK_NOPE + QK_ROPE
QK_PAD = 256
V_HEAD = 128
ROPE_THETA = 10000.0
LOG2E = math.log2(math.e)
Q_SCALE = QK_HEAD ** -0.5 * LOG2E
LANES = 128
SUBLANES = 8
BF16_ROWS = 16
F32_SAFE_EXP2 = 120.0

VMEM_LIMIT = 56 * 1024 * 1024


def _cparams(sem):
    return pltpu.CompilerParams(dimension_semantics=sem, vmem_limit_bytes=VMEM_LIMIT)


def _resident(shape):
    nd = len(shape)
    return pl.BlockSpec(shape, lambda *_: (0,) * nd, pipeline_mode=pl.Buffered(1))


def _rms_rows(x, gain):
    ms = jnp.mean(x * x, axis=-1, keepdims=True)
    return x * lax.rsqrt(ms + EPS) * gain


def _swiglu_residual(x, g_ref, wg_ref, wu_ref, wd_ref, chunk=FFN_CHUNK, side_work=None):
    hn = _rms_rows(x, g_ref[...]).astype(BF16)
    d_ff = wg_ref.shape[1]
    y = None
    for c0 in range(0, d_ff, chunk):
        c1 = min(c0 + chunk, d_ff)
        g = jnp.dot(hn, wg_ref[:, c0:c1], preferred_element_type=F32)
        u = jnp.dot(hn, wu_ref[:, c0:c1], preferred_element_type=F32)
        a = (g / (1.0 + jnp.exp(-g)) * u).astype(BF16)
        yc = jnp.dot(a, wd_ref[c0:c1, :], preferred_element_type=F32)
        y = yc if y is None else y + yc
        if side_work is not None:
            next(side_work, None)
    if side_work is not None:
        for _ in side_work:
            pass
    return x + FFN_HALF * y


def _cast_pieces(src_refs, dst_refs):
    for src, dst in zip(src_refs, dst_refs):
        dst[...] = src[...].astype(BF16)
        yield


def _cast_plan(next_w, n_steps):
    if next_w is None:
        return [], [], [], ()
    stacked, layer = next_w
    in_specs, out_specs, out_shapes = [], [], []
    for w in stacked:
        _, rows, cols = w.shape
        rb = next(r for r in range(BF16_ROWS, rows + 1, BF16_ROWS) if rows % r == 0 and rows // r <= n_steps)
        last = rows // rb - 1
        in_specs.append(pl.BlockSpec((None, rb, cols), lambda s, last=last: (layer, jnp.minimum(s, last), 0)))
        out_specs.append(pl.BlockSpec((rb, cols), lambda s, last=last: (jnp.minimum(s, last), 0)))
        out_shapes.append(jax.ShapeDtypeStruct((rows, cols), BF16))
    return in_specs, out_specs, out_shapes, tuple(stacked)


def _ffn_kernel(*refs, n_cast, with_proj):
    n_in = (7 if with_proj else 5) + n_cast
    ins, o_ref, cast_out = refs[:n_in - n_cast], refs[n_in], refs[n_in + 1:]
    cast_in = refs[n_in - n_cast:n_in]
    if with_proj:
        x_ref, a_ref, wo_ref, g_ref, wg_ref, wu_ref, wd_ref = ins
        x = x_ref[...] + jnp.dot(a_ref[...], wo_ref[...], preferred_element_type=F32)
    else:
        x_ref, g_ref, wg_ref, wu_ref, wd_ref = ins
        x = x_ref[...]
    o_ref[...] = _swiglu_residual(x, g_ref, wg_ref, wu_ref, wd_ref, side_work=_cast_pieces(cast_in, cast_out))


def _ffn(x2, gain, wg, wu, wd, tm, attn_out=None, w_out=None, next_w=None):
    T, D = x2.shape
    F = wg.shape[1]
    n_steps = T // tm
    row_tile = pl.BlockSpec((tm, D), lambda i: (i, 0))
    specs = [row_tile]
    args = (x2,)
    if attn_out is not None:
        specs += [pl.BlockSpec((tm, attn_out.shape[1]), lambda i: (i, 0)), _resident(w_out.shape)]
        args += (attn_out, w_out)
    specs += [_resident((1, D)), _resident((D, F)), _resident((D, F)), _resident((F, D))]
    args += (gain.reshape(1, D), wg, wu, wd)
    cast_in, cast_out, cast_shapes, cast_args = _cast_plan(next_w, n_steps)
    out = pl.pallas_call(
        functools.partial(_ffn_kernel, n_cast=len(cast_args), with_proj=attn_out is not None),
        grid=(n_steps,),
        in_specs=specs + cast_in,
        out_specs=[row_tile] + cast_out,
        out_shape=[jax.ShapeDtypeStruct((T, D), F32)] + cast_shapes,
        compiler_params=_cparams(("arbitrary",)),
        name="ffn" if attn_out is None else "proj_ffn",
    )(*args, *cast_args)
    return out[0], tuple(out[1:])


def _pool_mix(x, halo, first_of_seq, row0, g_ref, w_ref, sc_ref, hbuf, dst_ref):
    tm = x.shape[0]
    gain = g_ref[...]
    hn = _rms_rows(x, gain)
    hh = _rms_rows(halo, gain)
    hbuf[0:POOL_HALO] = jnp.where(first_of_seq, 0.0, hh)
    hbuf[POOL_HALO:POOL_HALO + tm] = hn
    yield
    C = w_ref.shape[1]
    rows = tm // 2
    for g, w_len in enumerate(POOL_WINDOWS):
        cols = slice(g * C, (g + 1) * C)
        for r0 in (0, rows):
            u = hbuf[POOL_HALO + r0:POOL_HALO + r0 + rows, cols]
            wsum = u
            for j in range(1, w_len):
                wsum = wsum + hbuf[POOL_HALO + r0 - j:POOL_HALO + r0 - j + rows, cols]
            t = row0 + r0 + lax.broadcasted_iota(jnp.int32, (rows, 1), 0)
            count = jnp.minimum(t + 1, w_len).astype(F32)
            pooled = (wsum / count - u).astype(BF16)
            y = jnp.dot(pooled, w_ref[g], preferred_element_type=F32)
            dst_ref[r0:r0 + rows, cols] = x[r0:r0 + rows, cols] + y * sc_ref[:, cols]
            yield


def _pool_ffn_kernel(*refs, n_tiles, tiles_per_seq, n_cast):
    x_ref, halo_ref, gmix_ref, pw_ref, psc_ref, g_ref, wg_ref, wu_ref, wd_ref = refs[:9]
    cast_in, o_ref, cast_out = refs[9:9 + n_cast], refs[9 + n_cast], refs[10 + n_cast:10 + 2 * n_cast]
    hbuf, xm0_ref, xm1_ref = refs[10 + 2 * n_cast:]
    s = pl.program_id(0)
    tm = x_ref.shape[0]

    def side(dst_ref=None):
        casts = _cast_pieces(cast_in, cast_out)
        if dst_ref is None:
            return casts
        i = s % tiles_per_seq
        mix = _pool_mix(x_ref[...], halo_ref[...], i == 0, i * tm, gmix_ref, pw_ref, psc_ref, hbuf, dst_ref)
        return itertools.chain(mix, casts)

    def ffn_from(src_ref, side_work):
        o_ref[...] = _swiglu_residual(src_ref[...], g_ref, wg_ref, wu_ref, wd_ref, chunk=POOL_FFN_CHUNK,
                                      side_work=side_work)

    last_ref = xm0_ref if (n_tiles - 1) % 2 == 0 else xm1_ref
    middle = jnp.logical_and(s > 0, s < n_tiles)

    @pl.when(s == 0)
    def _():
        for _ in side(xm0_ref):
            pass

    @pl.when(jnp.logical_and(middle, s % 2 == 1))
    def _():
        ffn_from(xm0_ref, side(xm1_ref))

    @pl.when(jnp.logical_and(middle, s % 2 == 0))
    def _():
        ffn_from(xm1_ref, side(xm0_ref))

    @pl.when(s == n_tiles)
    def _():
        ffn_from(last_ref, side())


def _pool_ffn(x2, seq_len, gmix, pool_w, pool_scale, gain, wg, wu, wd, tm, next_w=None):
    T, D = x2.shape
    F = wg.shape[1]
    G, C, _ = pool_w.shape
    n_tiles = T // tm
    hb = tm // POOL_HALO
    cast_in, cast_out, cast_shapes, cast_args = _cast_plan(next_w, n_tiles + 1)
    kern = functools.partial(_pool_ffn_kernel, n_tiles=n_tiles, tiles_per_seq=seq_len // tm, n_cast=len(cast_args))
    out = pl.pallas_call(
        kern,
        grid=(n_tiles + 1,),
        in_specs=[
            pl.BlockSpec((tm, D), lambda s: (jnp.minimum(s, n_tiles - 1), 0)),
            pl.BlockSpec((POOL_HALO, D), lambda s: (jnp.maximum(jnp.minimum(s, n_tiles - 1) * hb - 1, 0), 0)),
            _resident((1, D)),
            _resident((G, C, C)),
            _resident((1, D)),
            _resident((1, D)),
            _resident((D, F)),
            _resident((D, F)),
            _resident((F, D)),
        ] + cast_in,
        out_specs=[pl.BlockSpec((tm, D), lambda s: (jnp.maximum(s - 1, 0), 0))] + cast_out,
        out_shape=[jax.ShapeDtypeStruct((T, D), F32)] + cast_shapes,
        scratch_shapes=[pltpu.VMEM((POOL_HALO + tm, D), F32), pltpu.VMEM((tm, D), F32), pltpu.VMEM((tm, D), F32)],
        compiler_params=_cparams(("arbitrary",)),
        name="pool_ffn",
    )(x2, x2, gmix.reshape(1, D), pool_w, pool_scale.reshape(1, D), gain.reshape(1, D), wg, wu, wd, *cast_args)
    return out[0], tuple(out[1:])


def _rms_cols(xT, gain_col, n):
    ms = jnp.sum(xT * xT, axis=0, keepdims=True) * (1.0 / n)
    return xT * lax.rsqrt(ms + EPS) * gain_col


def _mla_pre_kernel(bound_ref, x_ref, pos_ref, invf_ref, gmix_ref, winT_ref, gq_ref, wqT_ref, gkv_ref, wkvT_ref,
                    gqh_ref, gkh_ref, qT_ref, k_ref, vT_ref, *, n_heads, q_lora, kv_lora, q_scale):
    tm = x_ref.shape[1]
    half = QK_ROPE // 2
    kv_head = QK_NOPE + V_HEAD

    def sub_tile(t0, tn):
        tok = slice(t0, t0 + tn)
        hn = _rms_rows(x_ref[0, tok, :], gmix_ref[...]).astype(BF16)
        latT = lax.dot_general(winT_ref[...], hn, (((1,), (1,)), ((), ())), preferred_element_type=F32)
        yield
        kpeT = latT[q_lora + kv_lora:]
        cqn = _rms_cols(latT[:q_lora], gq_ref[...], q_lora).astype(BF16)
        ckvn = _rms_cols(latT[q_lora:q_lora + kv_lora], gkv_ref[...], kv_lora).astype(BF16)
        ang = pos_ref[0, :, tok].astype(F32) * invf_ref[...]
        cos, sin = jnp.cos(ang), jnp.sin(ang)
        gqh = gqh_ref[...]
        gkh = gkh_ref[...]
        kpe_ss = jnp.sum(kpeT * kpeT, axis=0, keepdims=True)
        first = lax.broadcasted_iota(jnp.int32, (QK_PAD - QK_HEAD, tn), 0) == 0
        q_pad = jnp.where(first, 1.0, 0.0)
        k_pad = jnp.where(first, -bound_ref[0], 0.0)

        def rope(x1, x2):
            return x1 * cos - x2 * sin, x2 * cos + x1 * sin

        yield
        for h0 in range(0, n_heads, MLA_HEADS_PER_DOT):
            hs = MLA_HEADS_PER_DOT
            qg = jnp.dot(wqT_ref[h0 * QK_HEAD:(h0 + hs) * QK_HEAD, :], cqn, preferred_element_type=F32)
            kvg = jnp.dot(wkvT_ref[h0 * kv_head:(h0 + hs) * kv_head, :], ckvn, preferred_element_type=F32)
            for h in range(h0, h0 + hs):
                qh = qg[(h - h0) * QK_HEAD:(h - h0 + 1) * QK_HEAD]
                kvh = kvg[(h - h0) * kv_head:(h - h0 + 1) * kv_head]
                ms = jnp.sum(qh * qh, axis=0, keepdims=True) * (1.0 / QK_HEAD)
                qn = qh * (lax.rsqrt(ms + EPS) * q_scale) * gqh
                r1, r2 = rope(qn[QK_NOPE:QK_NOPE + half], qn[QK_NOPE + half:])
                qT_ref[0, h, :, tok] = jnp.concatenate([qn[:QK_NOPE], r1, r2, q_pad], axis=0).astype(BF16)

                kn = kvh[:QK_NOPE]
                ms = (jnp.sum(kn * kn, axis=0, keepdims=True) + kpe_ss) * (1.0 / QK_HEAD)
                rs = lax.rsqrt(ms + EPS)
                r1, r2 = rope(kpeT[:half] * rs * gkh[QK_NOPE:QK_NOPE + half],
                              kpeT[half:] * rs * gkh[QK_NOPE + half:])
                k_ref[0, h, 0, tok, :QK_NOPE] = (kn * rs * gkh[:QK_NOPE]).T.astype(BF16)
                k_ref[0, h, 0, tok, QK_NOPE:] = jnp.concatenate([r1, r2, k_pad], axis=0).T.astype(BF16)
                vT_ref[0, h, 0, :, tok] = kvh[QK_NOPE:].astype(BF16)
                yield

    n_sub = MLA_SUB_TILES if tm % (MLA_SUB_TILES * LANES) == 0 else 1
    stages = [sub_tile(i * (tm // n_sub), tm // n_sub) for i in range(n_sub)]
    live = list(stages)
    lead = 0
    while live:
        lead += 1
        for g in list(live[:lead]):
            if next(g, StopIteration) is StopIteration:
                live.remove(g)
                lead -= 1


def _mla_pre(bound, x3, pos3, invf, gmix, winT, gq, wqT, gkv, wkvT, gqh, gkh, n_heads, tm):
    B, S, D = x3.shape
    q_lora, kv_lora = wqT.shape[1], wkvT.shape[1]
    n_t = S // tm
    kern = functools.partial(_mla_pre_kernel, n_heads=n_heads, q_lora=q_lora, kv_lora=kv_lora,
                             q_scale=Q_SCALE)
    return pl.pallas_call(
        kern,
        grid=(B, n_t),
        in_specs=[
            pl.BlockSpec(memory_space=pltpu.SMEM),
            pl.BlockSpec((1, tm, D), lambda b, i: (b, i, 0)),
            pl.BlockSpec((1, 1, tm), lambda b, i: (b, 0, i)),
            _resident(invf.shape),
            _resident((1, D)),
            _resident(winT.shape),
            _resident((q_lora, 1)),
            _resident(wqT.shape),
            _resident((kv_lora, 1)),
            _resident(wkvT.shape),
            _resident((QK_HEAD, 1)),
            _resident((QK_HEAD, 1)),
        ],
        out_specs=[
            pl.BlockSpec((1, n_heads, QK_PAD, tm), lambda b, i: (b, 0, 0, i)),
            pl.BlockSpec((1, n_heads, 1, tm, QK_PAD), lambda b, i: (b, 0, i, 0, 0)),
            pl.BlockSpec((1, n_heads, 1, V_HEAD, tm), lambda b, i: (b, 0, i, 0, 0)),
        ],
        out_shape=[
            jax.ShapeDtypeStruct((B, n_heads, QK_PAD, S), BF16),
            jax.ShapeDtypeStruct((B, n_heads, n_t, tm, QK_PAD), BF16),
            jax.ShapeDtypeStruct((B, n_heads, n_t, V_HEAD, tm), BF16),
        ],
        compiler_params=_cparams(("parallel", "parallel")),
        name="mla_pre",
    )(bound, x3, pos3, invf, gmix.reshape(1, D), winT, gq.reshape(-1, 1), wqT, gkv.reshape(-1, 1), wkvT,
      gqh.reshape(-1, 1), gkh.reshape(-1, 1))


def _attn_kernel(safe_ref, qT_ref, k_ref, vT_ref, o_ref, m_ref, l_ref, acc_ref, p0_ref, p1_ref):
    i = pl.program_id(2)
    sub = l_ref.shape[0]
    tk, tq = p0_ref.shape
    v_w = vT_ref.shape[-1]
    l_ref[...] = jnp.zeros(l_ref.shape, F32)
    acc_ref[...] = jnp.zeros(acc_ref.shape, F32)

    def scores(j, diagonal):
        sT = jnp.dot(k_ref[0, 0, j], qT_ref[0, 0], preferred_element_type=F32)
        if not diagonal:
            return sT
        key = lax.broadcasted_iota(jnp.int32, sT.shape, 0)
        qry = lax.broadcasted_iota(jnp.int32, sT.shape, 1)
        return jnp.where(key <= qry, sT, -jnp.inf)

    def partial_sum(p):
        return jnp.sum(p.reshape(p.shape[0] // sub, sub, p.shape[1]), axis=0)

    def probs(j, p_ref):
        p = jnp.exp2(scores(j, False))
        l_ref[...] += partial_sum(p)
        p_ref[...] = p.astype(BF16)

    def probs_diagonal(j, p_ref):
        h = tk // 2

        def causal(sT):
            key = lax.broadcasted_iota(jnp.int32, sT.shape, 0)
            qry = lax.broadcasted_iota(jnp.int32, sT.shape, 1)
            return jnp.where(key <= qry, sT, -jnp.inf)

        kt = k_ref[0, 0, j]
        top = jnp.dot(kt[:h], qT_ref[0, 0], preferred_element_type=F32)
        p_top = jnp.exp2(causal(top))
        bot = jnp.dot(kt[h:], qT_ref[0, 0, :, h:], preferred_element_type=F32)
        p_bot = jnp.exp2(causal(bot))
        l_ref[...] += partial_sum(p_top)
        l_ref[:, h:] += partial_sum(p_bot)
        p_ref[:h, :] = p_top.astype(BF16)
        p_ref[h:, :h] = jnp.zeros((tk - h, h), BF16)
        p_ref[h:, h:] = p_bot.astype(BF16)

    def pv_value(j, p):
        out = None
        for c in range(tk // v_w):
            part = jnp.dot(vT_ref[0, 0, j * (tk // v_w) + c], p[c * v_w:(c + 1) * v_w, :],
                           preferred_element_type=F32)
            out = part if out is None else out + part
        return out

    def pv(j, p_ref):
        acc_ref[...] += pv_value(j, p_ref)

    def online(j, diagonal=False):
        sT = scores(j, diagonal)
        m_old = m_ref[...]
        m_new = jnp.maximum(m_old, jnp.max(sT, axis=0, keepdims=True))
        alpha = jnp.exp2(m_old - m_new)
        p = jnp.exp2(sT - m_new)
        l_ref[...] = alpha * l_ref[...] + partial_sum(p)
        acc_ref[...] = alpha * acc_ref[...] + pv_value(j, p.astype(BF16))
        m_ref[...] = m_new

    @pl.when(safe_ref[0] != 0)
    def _():
        probs_diagonal(i, p0_ref)

        def pair(t):
            probs(2 * t, p1_ref)
            pv(jnp.where(t == 0, i, 2 * t - 1), p0_ref)
            probs(2 * t + 1, p0_ref)
            pv(2 * t, p1_ref)

        def two_pairs(u, carry):
            pair(2 * u)
            pair(2 * u + 1)
            return carry

        n_pairs = i // 2
        lax.fori_loop(0, n_pairs // 2, two_pairs, 0)

        @pl.when(n_pairs % 2 == 1)
        def _():
            pair(n_pairs - 1)

        @pl.when(i % 2 == 0)
        def _():
            pv(jnp.maximum(i - 1, 0), p0_ref)

        @pl.when(i % 2 == 1)
        def _():
            probs(i - 1, p1_ref)
            pv(jnp.where(i == 1, i, i - 2), p0_ref)
            pv(i - 1, p1_ref)

    @pl.when(safe_ref[0] == 0)
    def _():
        m_ref[...] = jnp.full(m_ref.shape, -jnp.inf, F32)

        def body(j, carry):
            online(j)
            return carry

        lax.fori_loop(0, i, body, 0)
        online(i, diagonal=True)

    l = jnp.sum(l_ref[...], axis=0, keepdims=True)
    o_ref[0] = (acc_ref[...] / l).T.astype(o_ref.dtype)


def _attn(safe, qT, k, vT):
    B, H, _, S = qT.shape
    n_v, v_w = vT.shape[2], vT.shape[4]
    tq = ATTN_TILE if S % ATTN_TILE == 0 else v_w
    assert S % tq == 0 and tq % v_w == 0, (S, tq, v_w)
    k = k.reshape(B, H, S // tq, tq, QK_PAD)
    return pl.pallas_call(
        _attn_kernel,
        grid=(B, H, S // tq),
        in_specs=[
            pl.BlockSpec(memory_space=pltpu.SMEM),
            pl.BlockSpec((1, 1, QK_PAD, tq), lambda b, h, i: (b, h, 0, i)),
            pl.BlockSpec((1, 1, S // tq, tq, QK_PAD), lambda b, h, i: (b, h, 0, 0, 0)),
            pl.BlockSpec((1, 1, n_v, V_HEAD, v_w), lambda b, h, i: (b, h, 0, 0, 0)),
        ],
        out_specs=pl.BlockSpec((1, tq, V_HEAD), lambda b, h, i: (b, i, h)),
        out_shape=jax.ShapeDtypeStruct((B, S, H * V_HEAD), BF16),
        scratch_shapes=[pltpu.VMEM((1, tq), F32), pltpu.VMEM((SUBLANES, tq), F32),
                        pltpu.VMEM((V_HEAD, tq), F32), pltpu.VMEM((tq, tq), BF16), pltpu.VMEM((tq, tq), BF16)],
        compiler_params=_cparams(("parallel", "parallel", "arbitrary")),
        name="attn",
    )(safe, qT, k, vT)


def _tile(n, want):
    t = min(n, want)
    assert n % t == 0, (n, t)
    return t


def kernel(x, positions, ffn1_norm, ffn1_w_gate, ffn1_w_up, ffn1_w_down, mix_norm, pool_w, pool_scale,
           mla_w_in, mla_q_norm, mla_w_q_up, mla_kv_norm, mla_w_kv_up, mla_q_head_norm, mla_k_head_norm,
           mla_w_out, ffn2_norm, ffn2_w_gate, ffn2_w_up, ffn2_w_down):
    B, S, D = x.shape
    depth = ffn1_norm.shape[0]
    n_heads = mla_w_out.shape[1] // V_HEAD
    T = B * S
    tm_ffn = _tile(T, 1024)
    tm_pool = _tile(S, 512)
    tk = _tile(S, 1024)

    invf = (1.0 / (ROPE_THETA ** (jnp.arange(0, QK_ROPE, 2, dtype=F32) / QK_ROPE))).reshape(-1, 1)
    pos3 = positions.reshape(B, 1, S)

    ffn1_w = (ffn1_w_gate, ffn1_w_up, ffn1_w_down)
    ffn2_w = (ffn2_w_gate, ffn2_w_up, ffn2_w_down)
    w1 = tuple(w[0].astype(BF16) for w in ffn1_w)
    x = x.reshape(T, D)
    for i in range(depth):
        x, w2 = _ffn(x, ffn1_norm[i], *w1, tm_ffn, next_w=(ffn2_w, i))
        j = i // 2
        after = (ffn1_w, i + 1) if i + 1 < depth else None
        if i % 2 == 0:
            x, w1 = _pool_ffn(x, S, mix_norm[i], pool_w[j].astype(BF16), pool_scale[j], ffn2_norm[i], *w2,
                              tm_pool, next_w=after)
        else:
            bound = (Q_SCALE * QK_HEAD * jnp.max(jnp.abs(mla_q_head_norm[j]))
                     * jnp.max(jnp.abs(mla_k_head_norm[j]))).reshape(1)
            safe = (2.0 * bound <= F32_SAFE_EXP2).astype(jnp.int32)
            qT, k, vT = _mla_pre(bound, x.reshape(B, S, D), pos3, invf, mix_norm[i], mla_w_in[j].T.astype(BF16),
                                 mla_q_norm[j], mla_w_q_up[j].T.astype(BF16), mla_kv_norm[j],
                                 mla_w_kv_up[j].T.astype(BF16), mla_q_head_norm[j], mla_k_head_norm[j], n_heads, tk)
            o = _attn(safe, qT, k, vT)
            x, w1 = _ffn(x, ffn2_norm[i], *w2, tm_ffn, attn_out=o.reshape(T, n_heads * V_HEAD),
                         w_out=mla_w_out[j].astype(BF16), next_w=after)
    return x.reshape(B, S, D)
```

```python
import functools
import itertools
import math

import jax
import jax.numpy as jnp
from jax import lax
from jax.experimental import pallas as pl
from jax.experimental.pallas import tpu as pltpu

F32 = jnp.float32
BF16 = jnp.bfloat16

EPS = 1e-6
FFN_HALF = 0.5
FFN_CHUNK = 256
ATTN_Q_TILES_PER_STEP = 4
MLA_HEADS_PER_DOT = 2
MLA_SUB_TILES = 4
POOL_FFN_CHUNK = 256
POOL_WINDOWS = (2, 4, 8, 16)
POOL_HALO = 16
QK_NOPE = 128
QK_ROPE = 64
QK_HEAD = QK_NOPE + QK_ROPE
QK_PAD = 256
V_HEAD = 128
ROPE_THETA = 10000.0
LOG2E = math.log2(math.e)
Q_SCALE = QK_HEAD ** -0.5 * LOG2E
LANES = 128
SUBLANES = 8
BF16_ROWS = 16
F32_SAFE_EXP2 = 120.0

VMEM_LIMIT = 56 * 1024 * 1024


def _cparams(sem):
    return pltpu.CompilerParams(dimension_semantics=sem, vmem_limit_bytes=VMEM_LIMIT)


def _resident(shape):
    nd = len(shape)
    return pl.BlockSpec(shape, lambda *_: (0,) * nd, pipeline_mode=pl.Buffered(1))


def _rms_rows(x, gain):
    ms = jnp.mean(x * x, axis=-1, keepdims=True)
    return x * lax.rsqrt(ms + EPS) * gain


def _swiglu_residual(x, g_ref, wg_ref, wu_ref, wd_ref, chunk=FFN_CHUNK, side_work=None):
    hn = _rms_rows(x, g_ref[...]).astype(BF16)
    d_ff = wg_ref.shape[1]
    y = None
    for c0 in range(0, d_ff, chunk):
        c1 = min(c0 + chunk, d_ff)
        g = jnp.dot(hn, wg_ref[:, c0:c1], preferred_element_type=F32)
        u = jnp.dot(hn, wu_ref[:, c0:c1], preferred_element_type=F32)
        a = (g / (1.0 + jnp.exp(-g)) * u).astype(BF16)
        yc = jnp.dot(a, wd_ref[c0:c1, :], preferred_element_type=F32)
        y = yc if y is None else y + yc
        if side_work is not None:
            next(side_work, None)
    if side_work is not None:
        for _ in side_work:
            pass
    return x + FFN_HALF * y


def _cast_pieces(src_refs, dst_refs):
    for src, dst in zip(src_refs, dst_refs):
        dst[...] = src[...].astype(BF16)
        yield


def _cast_plan(next_w, n_steps):
    if next_w is None:
        return [], [], [], ()
    stacked, layer = next_w
    in_specs, out_specs, out_shapes = [], [], []
    for w in stacked:
        _, rows, cols = w.shape
        rb = next(r for r in range(BF16_ROWS, rows + 1, BF16_ROWS) if rows % r == 0 and rows // r <= n_steps)
        last = rows // rb - 1
        in_specs.append(pl.BlockSpec((None, rb, cols), lambda s, last=last: (layer, jnp.minimum(s, last), 0)))
        out_specs.append(pl.BlockSpec((rb, cols), lambda s, last=last: (jnp.minimum(s, last), 0)))
        out_shapes.append(jax.ShapeDtypeStruct((rows, cols), BF16))
    return in_specs, out_specs, out_shapes, tuple(stacked)


def _ffn_kernel(*refs, n_cast, with_proj):
    n_in = (7 if with_proj else 5) + n_cast
    ins, o_ref, cast_out = refs[:n_in - n_cast], refs[n_in], refs[n_in + 1:]
    cast_in = refs[n_in - n_cast:n_in]
    if with_proj:
        x_ref, a_ref, wo_ref, g_ref, wg_ref, wu_ref, wd_ref = ins
        x = x_ref[...] + jnp.dot(a_ref[...], wo_ref[...], preferred_element_type=F32)
    else:
        x_ref, g_ref, wg_ref, wu_ref, wd_ref = ins
        x = x_ref[...]
    o_ref[...] = _swiglu_residual(x, g_ref, wg_ref, wu_ref, wd_ref, side_work=_cast_pieces(cast_in, cast_out))


def _ffn(x2, gain, wg, wu, wd, tm, attn_out=None, w_out=None, next_w=None):
    T, D = x2.shape
    F = wg.shape[1]
    n_steps = T // tm
    row_tile = pl.BlockSpec((tm, D), lambda i: (i, 0))
    specs = [row_tile]
    args = (x2,)
    if attn_out is not None:
        specs += [pl.BlockSpec((tm, attn_out.shape[1]), lambda i: (i, 0)), _resident(w_out.shape)]
        args += (attn_out, w_out)
    specs += [_resident((1, D)), _resident((D, F)), _resident((D, F)), _resident((F, D))]
    args += (gain.reshape(1, D), wg, wu, wd)
    cast_in, cast_out, cast_shapes, cast_args = _cast_plan(next_w, n_steps)
    out = pl.pallas_call(
        functools.partial(_ffn_kernel, n_cast=len(cast_args), with_proj=attn_out is not None),
        grid=(n_steps,),
        in_specs=specs + cast_in,
        out_specs=[row_tile] + cast_out,
        out_shape=[jax.ShapeDtypeStruct((T, D), F32)] + cast_shapes,
        compiler_params=_cparams(("arbitrary",)),
        name="ffn" if attn_out is None else "proj_ffn",
    )(*args, *cast_args)
    return out[0], tuple(out[1:])


def _pool_mix(x, halo, first_of_seq, row0, g_ref, w_ref, sc_ref, hbuf, dst_ref):
    tm = x.shape[0]
    gain = g_ref[...]
    hn = _rms_rows(x, gain)
    hh = _rms_rows(halo, gain)
    hbuf[0:POOL_HALO] = jnp.where(first_of_seq, 0.0, hh)
    hbuf[POOL_HALO:POOL_HALO + tm] = hn
    yield
    C = w_ref.shape[1]
    rows = tm // 2
    for g, w_len in enumerate(POOL_WINDOWS):
        cols = slice(g * C, (g + 1) * C)
        for r0 in (0, rows):
            u = hbuf[POOL_HALO + r0:POOL_HALO + r0 + rows, cols]
            wsum = u
            for j in range(1, w_len):
                wsum = wsum + hbuf[POOL_HALO + r0 - j:POOL_HALO + r0 - j + rows, cols]
            t = row0 + r0 + lax.broadcasted_iota(jnp.int32, (rows, 1), 0)
            count = jnp.minimum(t + 1, w_len).astype(F32)
            pooled = (wsum / count - u).astype(BF16)
            y = jnp.dot(pooled, w_ref[g], preferred_element_type=F32)
            dst_ref[r0:r0 + rows, cols] = x[r0:r0 + rows, cols] + y * sc_ref[:, cols]
            yield


def _pool_ffn_kernel(*refs, n_tiles, tiles_per_seq, n_cast):
    x_ref, halo_ref, gmix_ref, pw_ref, psc_ref, g_ref, wg_ref, wu_ref, wd_ref = refs[:9]
    cast_in, o_ref, cast_out = refs[9:9 + n_cast], refs[9 + n_cast], refs[10 + n_cast:10 + 2 * n_cast]
    hbuf, xm0_ref, xm1_ref = refs[10 + 2 * n_cast:]
    s = pl.program_id(0)
    tm = x_ref.shape[0]

    def side(dst_ref=None):
        casts = _cast_pieces(cast_in, cast_out)
        if dst_ref is None:
            return casts
        i = s % tiles_per_seq
        mix = _pool_mix(x_ref[...], halo_ref[...], i == 0, i * tm, gmix_ref, pw_ref, psc_ref, hbuf, dst_ref)
        return itertools.chain(mix, casts)

    def ffn_from(src_ref, side_work):
        o_ref[...] = _swiglu_residual(src_ref[...], g_ref, wg_ref, wu_ref, wd_ref, chunk=POOL_FFN_CHUNK,
                                      side_work=side_work)

    last_ref = xm0_ref if (n_tiles - 1) % 2 == 0 else xm1_ref
    middle = jnp.logical_and(s > 0, s < n_tiles)

    @pl.when(s == 0)
    def _():
        for _ in side(xm0_ref):
            pass

    @pl.when(jnp.logical_and(middle, s % 2 == 1))
    def _():
        ffn_from(xm0_ref, side(xm1_ref))

    @pl.when(jnp.logical_and(middle, s % 2 == 0))
    def _():
        ffn_from(xm1_ref, side(xm0_ref))

    @pl.when(s == n_tiles)
    def _():
        ffn_from(last_ref, side())


def _pool_ffn(x2, seq_len, gmix, pool_w, pool_scale, gain, wg, wu, wd, tm, next_w=None):
    T, D = x2.shape
    F = wg.shape[1]
    G, C, _ = pool_w.shape
    n_tiles = T // tm
    hb = tm // POOL_HALO
    cast_in, cast_out, cast_shapes, cast_args = _cast_plan(next_w, n_tiles + 1)
    kern = functools.partial(_pool_ffn_kernel, n_tiles=n_tiles, tiles_per_seq=seq_len // tm, n_cast=len(cast_args))
    out = pl.pallas_call(
        kern,
        grid=(n_tiles + 1,),
        in_specs=[
            pl.BlockSpec((tm, D), lambda s: (jnp.minimum(s, n_tiles - 1), 0)),
            pl.BlockSpec((POOL_HALO, D), lambda s: (jnp.maximum(jnp.minimum(s, n_tiles - 1) * hb - 1, 0), 0)),
            _resident((1, D)),
            _resident((G, C, C)),
            _resident((1, D)),
            _resident((1, D)),
            _resident((D, F)),
            _resident((D, F)),
            _resident((F, D)),
        ] + cast_in,
        out_specs=[pl.BlockSpec((tm, D), lambda s: (jnp.maximum(s - 1, 0), 0))] + cast_out,
        out_shape=[jax.ShapeDtypeStruct((T, D), F32)] + cast_shapes,
        scratch_shapes=[pltpu.VMEM((POOL_HALO + tm, D), F32), pltpu.VMEM((tm, D), F32), pltpu.VMEM((tm, D), F32)],
        compiler_params=_cparams(("arbitrary",)),
        name="pool_ffn",
    )(x2, x2, gmix.reshape(1, D), pool_w, pool_scale.reshape(1, D), gain.reshape(1, D), wg, wu, wd, *cast_args)
    return out[0], tuple(out[1:])


def _rms_cols(xT, gain_col, n):
    ms = jnp.sum(xT * xT, axis=0, keepdims=True) * (1.0 / n)
    return xT * lax.rsqrt(ms + EPS) * gain_col


def _mla_pre_kernel(bound_ref, x_ref, pos_ref, invf_ref, gmix_ref, winT_ref, gq_ref, wqT_ref, gkv_ref, wkvT_ref,
                    gqh_ref, gkh_ref, qT_ref, k_ref, vT_ref, *, n_heads, q_lora, kv_lora, q_scale):
    tm = x_ref.shape[1]
    half = QK_ROPE // 2
    kv_head = QK_NOPE + V_HEAD

    def sub_tile(t0, tn):
        tok = slice(t0, t0 + tn)
        hn = _rms_rows(x_ref[0, tok, :], gmix_ref[...]).astype(BF16)
        latT = lax.dot_general(winT_ref[...], hn, (((1,), (1,)), ((), ())), preferred_element_type=F32)
        yield
        kpeT = latT[q_lora + kv_lora:]
        cqn = _rms_cols(latT[:q_lora], gq_ref[...], q_lora).astype(BF16)
        ckvn = _rms_cols(latT[q_lora:q_lora + kv_lora], gkv_ref[...], kv_lora).astype(BF16)
        ang = pos_ref[0, :, tok].astype(F32) * invf_ref[...]
        cos, sin = jnp.cos(ang), jnp.sin(ang)
        gqh = gqh_ref[...]
        gkh = gkh_ref[...]
        kpe_ss = jnp.sum(kpeT * kpeT, axis=0, keepdims=True)
        first = lax.broadcasted_iota(jnp.int32, (QK_PAD - QK_HEAD, tn), 0) == 0
        q_pad = jnp.where(first, 1.0, 0.0)
        k_pad = jnp.where(first, -bound_ref[0], 0.0)

        def rope(x1, x2):
            return x1 * cos - x2 * sin, x2 * cos + x1 * sin

        yield
        for h0 in range(0, n_heads, MLA_HEADS_PER_DOT):
            hs = MLA_HEADS_PER_DOT
            qg = jnp.dot(wqT_ref[h0 * QK_HEAD:(h0 + hs) * QK_HEAD, :], cqn, preferred_element_type=F32)
            kvg = jnp.dot(wkvT_ref[h0 * kv_head:(h0 + hs) * kv_head, :], ckvn, preferred_element_type=F32)
            for h in range(h0, h0 + hs):
                qh = qg[(h - h0) * QK_HEAD:(h - h0 + 1) * QK_HEAD]
                kvh = kvg[(h - h0) * kv_head:(h - h0 + 1) * kv_head]
                ms = jnp.sum(qh * qh, axis=0, keepdims=True) * (1.0 / QK_HEAD)
                qn = qh * (lax.rsqrt(ms + EPS) * q_scale) * gqh
                r1, r2 = rope(qn[QK_NOPE:QK_NOPE + half], qn[QK_NOPE + half:])
                qT_ref[0, h, 0, :, tok] = jnp.concatenate([qn[:QK_NOPE], r1, r2, q_pad], axis=0).astype(BF16)

                kn = kvh[:QK_NOPE]
                ms = (jnp.sum(kn * kn, axis=0, keepdims=True) + kpe_ss) * (1.0 / QK_HEAD)
                rs = lax.rsqrt(ms + EPS)
                r1, r2 = rope(kpeT[:half] * rs * gkh[QK_NOPE:QK_NOPE + half],
                              kpeT[half:] * rs * gkh[QK_NOPE + half:])
                k_ref[0, h, 0, tok, :QK_NOPE] = (kn * rs * gkh[:QK_NOPE]).T.astype(BF16)
                k_ref[0, h, 0, tok, QK_NOPE:] = jnp.concatenate([r1, r2, k_pad], axis=0).T.astype(BF16)
                vT_ref[0, h, 0, :, tok] = kvh[QK_NOPE:].astype(BF16)
                yield

    n_sub = MLA_SUB_TILES if tm % (MLA_SUB_TILES * LANES) == 0 else 1
    stages = [sub_tile(i * (tm // n_sub), tm // n_sub) for i in range(n_sub)]
    live = list(stages)
    lead = 0
    while live:
        lead += 1
        for g in list(live[:lead]):
            if next(g, StopIteration) is StopIteration:
                live.remove(g)
                lead -= 1


def _mla_pre(bound, x3, pos3, invf, gmix, winT, gq, wqT, gkv, wkvT, gqh, gkh, n_heads, tm):
    B, S, D = x3.shape
    q_lora, kv_lora = wqT.shape[1], wkvT.shape[1]
    n_t = S // tm
    kern = functools.partial(_mla_pre_kernel, n_heads=n_heads, q_lora=q_lora, kv_lora=kv_lora,
                             q_scale=Q_SCALE)
    return pl.pallas_call(
        kern,
        grid=(B, n_t),
        in_specs=[
            pl.BlockSpec(memory_space=pltpu.SMEM),
            pl.BlockSpec((1, tm, D), lambda b, i: (b, i, 0)),
            pl.BlockSpec((1, 1, tm), lambda b, i: (b, 0, i)),
            _resident(invf.shape),
            _resident((1, D)),
            _resident(winT.shape),
            _resident((q_lora, 1)),
            _resident(wqT.shape),
            _resident((kv_lora, 1)),
            _resident(wkvT.shape),
            _resident((QK_HEAD, 1)),
            _resident((QK_HEAD, 1)),
        ],
        out_specs=[
            pl.BlockSpec((1, n_heads, 1, QK_PAD, tm), lambda b, i: (b, 0, i, 0, 0)),
            pl.BlockSpec((1, n_heads, 1, tm, QK_PAD), lambda b, i: (b, 0, i, 0, 0)),
            pl.BlockSpec((1, n_heads, 1, V_HEAD, tm), lambda b, i: (b, 0, i, 0, 0)),
        ],
        out_shape=[
            jax.ShapeDtypeStruct((B, n_heads, n_t, QK_PAD, tm), BF16),
            jax.ShapeDtypeStruct((B, n_heads, n_t, tm, QK_PAD), BF16),
            jax.ShapeDtypeStruct((B, n_heads, n_t, V_HEAD, tm), BF16),
        ],
        compiler_params=_cparams(("parallel", "parallel")),
        name="mla_pre",
    )(bound, x3, pos3, invf, gmix.reshape(1, D), winT, gq.reshape(-1, 1), wqT, gkv.reshape(-1, 1), wkvT,
      gqh.reshape(-1, 1), gkh.reshape(-1, 1))


def _attn_tile(g, i, safe_ref, qT_ref, k_ref, vT_ref, o_ref, m_ref, l_ref, acc_ref, p0_ref, p1_ref):
    sub = l_ref.shape[0]
    tk, tq = p0_ref.shape
    v_w = vT_ref.shape[-1]
    l_ref[...] = jnp.zeros(l_ref.shape, F32)
    acc_ref[...] = jnp.zeros(acc_ref.shape, F32)

    def scores(j, diagonal):
        sT = jnp.dot(k_ref[0, 0, j], qT_ref[0, 0, g], preferred_element_type=F32)
        if not diagonal:
            return sT
        key = lax.broadcasted_iota(jnp.int32, sT.shape, 0)
        qry = lax.broadcasted_iota(jnp.int32, sT.shape, 1)
        return jnp.where(key <= qry, sT, -jnp.inf)

    def partial_sum(p):
        return jnp.sum(p.reshape(p.shape[0] // sub, sub, p.shape[1]), axis=0)

    def probs(j, p_ref):
        p = jnp.exp2(scores(j, False))
        l_ref[...] += partial_sum(p)
        p_ref[...] = p.astype(BF16)

    def probs_diagonal(j, p_ref):
        h = tk // 2

        def causal(sT):
            key = lax.broadcasted_iota(jnp.int32, sT.shape, 0)
            qry = lax.broadcasted_iota(jnp.int32, sT.shape, 1)
            return jnp.where(key <= qry, sT, -jnp.inf)

        kt = k_ref[0, 0, j]
        top = jnp.dot(kt[:h], qT_ref[0, 0, g], preferred_element_type=F32)
        p_top = jnp.exp2(causal(top))
        bot = jnp.dot(kt[h:], qT_ref[0, 0, g, :, h:], preferred_element_type=F32)
        p_bot = jnp.exp2(causal(bot))
        l_ref[...] += partial_sum(p_top)
        l_ref[:, h:] += partial_sum(p_bot)
        p_ref[:h, :] = p_top.astype(BF16)
        p_ref[h:, :h] = jnp.zeros((tk - h, h), BF16)
        p_ref[h:, h:] = p_bot.astype(BF16)

    def pv_value(j, p):
        out = None
        for c in range(tk // v_w):
            part = jnp.dot(vT_ref[0, 0, j * (tk // v_w) + c], p[c * v_w:(c + 1) * v_w, :],
                           preferred_element_type=F32)
            out = part if out is None else out + part
        return out

    def pv(j, p_ref):
        acc_ref[...] += pv_value(j, p_ref)

    def online(j, diagonal=False):
        sT = scores(j, diagonal)
        m_old = m_ref[...]
        m_new = jnp.maximum(m_old, jnp.max(sT, axis=0, keepdims=True))
        alpha = jnp.exp2(m_old - m_new)
        p = jnp.exp2(sT - m_new)
        l_ref[...] = alpha * l_ref[...] + partial_sum(p)
        acc_ref[...] = alpha * acc_ref[...] + pv_value(j, p.astype(BF16))
        m_ref[...] = m_new

    @pl.when(safe_ref[0] != 0)
    def _():
        probs_diagonal(i, p0_ref)

        def pair(t):
            probs(2 * t, p1_ref)
            pv(jnp.where(t == 0, i, 2 * t - 1), p0_ref)
            probs(2 * t + 1, p0_ref)
            pv(2 * t, p1_ref)

        def two_pairs(u, carry):
            pair(2 * u)
            pair(2 * u + 1)
            return carry

        n_pairs = i // 2
        lax.fori_loop(0, n_pairs // 2, two_pairs, 0)

        @pl.when(n_pairs % 2 == 1)
        def _():
            pair(n_pairs - 1)

        @pl.when(i % 2 == 0)
        def _():
            pv(jnp.maximum(i - 1, 0), p0_ref)

        @pl.when(i % 2 == 1)
        def _():
            probs(i - 1, p1_ref)
            pv(jnp.where(i == 1, i, i - 2), p0_ref)
            pv(i - 1, p1_ref)

    @pl.when(safe_ref[0] == 0)
    def _():
        m_ref[...] = jnp.full(m_ref.shape, -jnp.inf, F32)

        def body(j, carry):
            online(j)
            return carry

        lax.fori_loop(0, i, body, 0)
        online(i, diagonal=True)

    l = jnp.sum(l_ref[...], axis=0, keepdims=True)
    o_ref[0, pl.ds(pl.multiple_of(g * tq, tq), tq), :] = (acc_ref[...] / l).T.astype(o_ref.dtype)


def _attn_kernel(*refs, q_tiles):
    step = pl.program_id(2)

    def body(g, carry):
        _attn_tile(g, step * q_tiles + g, *refs)
        return carry

    lax.fori_loop(0, q_tiles, body, 0)


def _attn(safe, qT, k, vT):
    B, H, n_q, _, tq = qT.shape
    S = n_q * tq
    n_v, v_w = vT.shape[2], vT.shape[4]
    assert tq % v_w == 0, (tq, v_w)
    q_tiles = math.gcd(n_q, ATTN_Q_TILES_PER_STEP)
    k = k.reshape(B, H, n_q, tq, QK_PAD)
    return pl.pallas_call(
        functools.partial(_attn_kernel, q_tiles=q_tiles),
        grid=(B, H, n_q // q_tiles),
        in_specs=[
            pl.BlockSpec(memory_space=pltpu.SMEM),
            pl.BlockSpec((1, 1, q_tiles, QK_PAD, tq), lambda b, h, i: (b, h, i, 0, 0)),
            pl.BlockSpec((1, 1, n_q, tq, QK_PAD), lambda b, h, i: (b, h, 0, 0, 0)),
            pl.BlockSpec((1, 1, n_v, V_HEAD, v_w), lambda b, h, i: (b, h, 0, 0, 0)),
        ],
        out_specs=pl.BlockSpec((1, q_tiles * tq, V_HEAD), lambda b, h, i: (b, i, h)),
        out_shape=jax.ShapeDtypeStruct((B, S, H * V_HEAD), BF16),
        scratch_shapes=[pltpu.VMEM((1, tq), F32), pltpu.VMEM((SUBLANES, tq), F32),
                        pltpu.VMEM((V_HEAD, tq), F32), pltpu.VMEM((tq, tq), BF16), pltpu.VMEM((tq, tq), BF16)],
        compiler_params=_cparams(("parallel", "parallel", "arbitrary")),
        name="attn",
    )(safe, qT, k, vT)


def _tile(n, want):
    t = min(n, want)
    assert n % t == 0, (n, t)
    return t


def kernel(x, positions, ffn1_norm, ffn1_w_gate, ffn1_w_up, ffn1_w_down, mix_norm, pool_w, pool_scale,
           mla_w_in, mla_q_norm, mla_w_q_up, mla_kv_norm, mla_w_kv_up, mla_q_head_norm, mla_k_head_norm,
           mla_w_out, ffn2_norm, ffn2_w_gate, ffn2_w_up, ffn2_w_down):
    B, S, D = x.shape
    depth = ffn1_norm.shape[0]
    n_heads = mla_w_out.shape[1] // V_HEAD
    T = B * S
    tm_ffn = _tile(T, 1024)
    tm_pool = _tile(S, 512)
    tk = _tile(S, 1024)

    invf = (1.0 / (ROPE_THETA ** (jnp.arange(0, QK_ROPE, 2, dtype=F32) / QK_ROPE))).reshape(-1, 1)
    pos3 = positions.reshape(B, 1, S)

    ffn1_w = (ffn1_w_gate, ffn1_w_up, ffn1_w_down)
    ffn2_w = (ffn2_w_gate, ffn2_w_up, ffn2_w_down)
    w1 = tuple(w[0].astype(BF16) for w in ffn1_w)
    x = x.reshape(T, D)
    for i in range(depth):
        x, w2 = _ffn(x, ffn1_norm[i], *w1, tm_ffn, next_w=(ffn2_w, i))
        j = i // 2
        after = (ffn1_w, i + 1) if i + 1 < depth else None
        if i % 2 == 0:
            x, w1 = _pool_ffn(x, S, mix_norm[i], pool_w[j].astype(BF16), pool_scale[j], ffn2_norm[i], *w2,
                              tm_pool, next_w=after)
        else:
            bound = (Q_SCALE * QK_HEAD * jnp.max(jnp.abs(mla_q_head_norm[j]))
                     * jnp.max(jnp.abs(mla_k_head_norm[j]))).reshape(1)
            safe = (2.0 * bound <= F32_SAFE_EXP2).astype(jnp.int32)
            qT, k, vT = _mla_pre(bound, x.reshape(B, S, D), pos3, invf, mix_norm[i], mla_w_in[j].T.astype(BF16),
                                 mla_q_norm[j], mla_w_q_up[j].T.astype(BF16), mla_kv_norm[j],
                                 mla_w_kv_up[j].T.astype(BF16), mla_q_head_norm[j], mla_k_head_norm[j], n_heads, tk)
            o = _attn(safe, qT, k, vT)
            x, w1 = _ffn(x, ffn2_norm[i], *w2, tm_ffn, attn_out=o.reshape(T, n_heads * V_HEAD),
                         w_out=mla_w_out[j].astype(BF16), next_w=after)
    return x.reshape(B, S, D)
```
